```python
import math
import jax, jax.numpy as jnp
from jax import lax
import numpy as np

D_MODEL = 2048
BATCH = 2
SEQ = 16384
DEPTH = 2
DEC_BATCH = 8
DEC_SEQ = 64
PAST_LEN = 2048

CHUNK = 64
Q_BLOCK = 128
CONV_WIDTH = 31
D_CONV = 512
N_DIFF = 8
DIFF_HD = 64
DIFF_VD = 2 * DIFF_HD
N_RET = 4
RET_HD = 128
D_FF = 5632
N_BUCKETS = 32
MAX_DISTANCE = 128
LN_EPS = 1e-5
ROPE_BASE = 10000.0
ALPHA = (2 * DEPTH) ** 0.25
BETA = (8 * DEPTH) ** -0.25
D_MIX = D_CONV + N_DIFF * DIFF_VD + N_RET * RET_HD
_SIZES = (D_CONV, D_CONV,
          N_DIFF * 2 * DIFF_HD, N_DIFF * 2 * DIFF_HD, N_DIFF * DIFF_VD,
          N_RET * RET_HD, N_RET * RET_HD, N_RET * RET_HD, N_RET * RET_HD)
D_IN = sum(_SIZES)
SPLIT_POINTS = tuple(int(s) for s in np.cumsum(_SIZES)[:-1])

kernel_name = "hybrid_conv_diffattn_retention_stream_step"


def _layer_norm(x, g, b):
    xf = x.astype(jnp.float32)
    mu = jnp.mean(xf, axis=-1, keepdims=True)
    var = jnp.mean(jnp.square(xf - mu), axis=-1, keepdims=True)
    return ((xf - mu) * lax.rsqrt(var + LN_EPS) * g.astype(jnp.float32) + b.astype(jnp.float32)).astype(x.dtype)


def _swiglu(x, wg, wu, wd):
    return (jax.nn.silu(x @ wg) * (x @ wu)) @ wd


def _rel_bucket(rel):
    half = N_BUCKETS // 2
    max_exact = half // 2
    ret = jnp.where(rel > 0, half, 0)
    n = jnp.abs(rel)
    nf = jnp.maximum(n, 1).astype(jnp.float32)
    large = max_exact + (jnp.log(nf / max_exact) / math.log(MAX_DISTANCE / max_exact)
                         * (half - max_exact)).astype(jnp.int32)
    large = jnp.minimum(large, half - 1)
    return ret + jnp.where(n < max_exact, n, large)


def _diff_attn_block(q, qpos, k, v, kpos, lam, rel_bias):
    B, Lk = k.shape[0], k.shape[1]
    k2 = k.reshape(B, Lk, N_DIFF, 2, DIFF_HD)
    s = jnp.einsum('bqhmd,bkhmd->bmhqk', q, k2).astype(jnp.float32) * (DIFF_HD ** -0.5)
    bias = rel_bias[_rel_bucket(kpos[None, :] - qpos[:, None])]
    bias = jnp.transpose(bias, (2, 0, 1)).astype(jnp.float32)
    allowed = (kpos[None, :] // CHUNK) <= (qpos[:, None] // CHUNK)
    s = jnp.where(allowed, s + bias, -jnp.inf)
    p = jax.nn.softmax(s, axis=-1)
    a = p[:, 0] - lam * p[:, 1]
    return jnp.einsum('bhqk,bkhe->bqhe', a.astype(v.dtype), v)


def _diff_attention(q, k, v, q_pos, k_pos, lam, lam_init, sub_g, rel_bias):
    B, Lq = q.shape[0], q.shape[1]
    if Lq <= Q_BLOCK:
        o = _diff_attn_block(q, q_pos, k, v, k_pos, lam, rel_bias)
    else:
        nb = Lq // Q_BLOCK
        qb = jnp.swapaxes(q.reshape(B, nb, Q_BLOCK, N_DIFF, 2, DIFF_HD), 0, 1)
        pb = q_pos.reshape(nb, Q_BLOCK)
        o = lax.map(lambda a: _diff_attn_block(a[0], a[1], k, v, k_pos, lam, rel_bias), (qb, pb))
        o = jnp.swapaxes(o, 0, 1).reshape(B, Lq, N_DIFF, DIFF_VD)
    of = o.astype(jnp.float32)
    of = of * lax.rsqrt(jnp.mean(jnp.square(of), axis=-1, keepdims=True) + LN_EPS) * sub_g.astype(jnp.float32)
    return (of * (1.0 - lam_init)).astype(q.dtype).reshape(B, Lq, N_DIFF * DIFF_VD)


def _rope(x, pos):
    half = x.shape[-1] // 2
    inv = 1.0 / (ROPE_BASE ** (jnp.arange(half, dtype=jnp.float32) / half))
    ang = pos.astype(jnp.float32)[:, None] * inv[None, :]
    cos = jnp.cos(ang)[None, :, None, :]
    sin = jnp.sin(ang)[None, :, None, :]
    x1, x2 = x[..., :half], x[..., half:]
    return jnp.concatenate([x1 * cos - x2 * sin, x1 * sin + x2 * cos], axis=-1)


def _retention(q, k, v, s0, L):
    B, T, H, D = q.shape
    n = T // L
    log_g = jnp.log1p(-jnp.exp2(-5.0 - jnp.arange(H, dtype=jnp.float32)))
    idx = jnp.arange(L, dtype=jnp.float32)
    rel = idx[:, None] - idx[None, :]
    decay = jnp.where(rel >= 0, jnp.exp(jnp.maximum(rel, 0.0)[None] * log_g[:, None, None]), 0.0)
    qc = q.reshape(B, n, L, H, D)
    kc = k.reshape(B, n, L, H, D)
    vc = v.reshape(B, n, L, H, D)
    inner = jnp.einsum('bnihd,bnjhd->bnhij', qc, kc) * decay
    o_inner = jnp.einsum('bnhij,bnjhe->bnihe', inner, vc)
    w_k = jnp.exp((L - 1.0 - idx)[None, :] * log_g[:, None])
    kv = jnp.einsum('bnjhd,hj,bnjhe->bnhde', kc, w_k, vc)
    g_L = jnp.exp(L * log_g)[None, :, None, None]

    def step(s, kv_n):
        return g_L * s + kv_n, s

    s_last, s_prev = lax.scan(step, s0, jnp.swapaxes(kv, 0, 1))
    w_q = jnp.exp((idx + 1.0)[None, :] * log_g[:, None])
    o_cross = jnp.einsum('bnihd,hi,nbhde->bnihe', qc, w_q, s_prev)
    return (o_inner + o_cross).reshape(B, T, H, D), s_last


def _conv_module(a, b, prev, w, cb, g, beta):
    u = a * jax.nn.sigmoid(b)
    ext = jnp.concatenate([prev.astype(u.dtype), u], axis=1)
    y = lax.conv_general_dilated(ext, w[:, None, :].astype(u.dtype), window_strides=(1,), padding='VALID',
                                 dimension_numbers=('NWC', 'WIO', 'NWC'), feature_group_count=D_CONV)
    y = jax.nn.silu(_layer_norm(y + cb, g, beta))
    return y, ext[:, -(CONV_WIDTH - 1):]


def _mixer(h, pos0, conv_prev, ret_prev, k_past, v_past, ret_chunk, lam_init,
           w_in, w_out, conv_w, conv_b, conv_ln_g, conv_ln_b, diff_lam, diff_sub_g, rel_bias):
    B, T, _ = h.shape
    proj = h @ w_in
    ca, cg, dq, dk, dv, rq, rk, rv, rg = jnp.split(proj, SPLIT_POINTS, axis=-1)
    conv_out, conv_new = _conv_module(ca, cg, conv_prev, conv_w, conv_b, conv_ln_g, conv_ln_b)
    q_pos = pos0 + jnp.arange(T, dtype=jnp.int32)
    dq = dq.reshape(B, T, N_DIFF, 2, DIFF_HD)
    dk = dk.reshape(B, T, N_DIFF, 2 * DIFF_HD)
    dv = dv.reshape(B, T, N_DIFF, DIFF_VD)
    if k_past is None:
        k_all, v_all, k_pos = dk, dv, q_pos
    else:
        k_all = jnp.concatenate([k_past.astype(dk.dtype), dk], axis=1)
        v_all = jnp.concatenate([v_past.astype(dv.dtype), dv], axis=1)
        k_pos = jnp.arange(k_past.shape[1] + T, dtype=jnp.int32)
    lf = diff_lam.astype(jnp.float32)
    lam = jnp.exp(jnp.sum(lf[0] * lf[1])) - jnp.exp(jnp.sum(lf[2] * lf[3])) + lam_init
    attn_out = _diff_attention(dq, k_all, v_all, q_pos, k_pos, lam, lam_init, diff_sub_g, rel_bias)
    rq = _rope(rq.reshape(B, T, N_RET, RET_HD).astype(jnp.float32), q_pos)
    rk = _rope(rk.reshape(B, T, N_RET, RET_HD).astype(jnp.float32), q_pos) * (RET_HD ** -0.5)
    rv = rv.reshape(B, T, N_RET, RET_HD).astype(jnp.float32)
    o, s_new = _retention(rq, rk, rv, ret_prev.astype(jnp.float32), ret_chunk)
    mu = jnp.mean(o, axis=-1, keepdims=True)
    var = jnp.mean(jnp.square(o - mu), axis=-1, keepdims=True)
    o = ((o - mu) * lax.rsqrt(var + LN_EPS)).reshape(B, T, N_RET * RET_HD)
    ret_out = (jax.nn.silu(rg.astype(jnp.float32)) * o).astype(h.dtype)
    mix = jnp.concatenate([conv_out, attn_out, ret_out], axis=-1) @ w_out
    return mix, conv_new, dk, dv, s_new


def _trunk(x, pos0, cache_conv, cache_k, cache_v, state_ret, ln_g, ln_b, ffn_w_gate, ffn_w_up, ffn_w_down,
           w_in, w_out, conv_w, conv_b, conv_ln_g, conv_ln_b, diff_lam, diff_sub_g, rel_bias):
    B, T, _ = x.shape
    ret_chunk = CHUNK if cache_k is None else T
    conv_l, k_l, v_l, s_l = [], [], [], []
    for l in range(DEPTH):
        lam_init = 0.8 - 0.6 * math.exp(-0.3 * l)
        if cache_k is None:
            conv_prev = jnp.zeros((B, CONV_WIDTH - 1, D_CONV), x.dtype)
            ret_prev = jnp.zeros((B, N_RET, RET_HD, RET_HD), jnp.float32)
            k_past = None
            v_past = None
        else:
            conv_prev, ret_prev, k_past, v_past = cache_conv[l], state_ret[l], cache_k[l], cache_v[l]
        x = _layer_norm(ALPHA * x + 0.5 * _swiglu(x, ffn_w_gate[l, 0], ffn_w_up[l, 0], ffn_w_down[l, 0]),
                        ln_g[l, 0], ln_b[l, 0])
        mix, c_new, k_new, v_new, s_new = _mixer(x, pos0, conv_prev, ret_prev, k_past, v_past, ret_chunk, lam_init,
                                                 w_in[l], w_out[l], conv_w[l], conv_b[l], conv_ln_g[l],
                                                 conv_ln_b[l], diff_lam[l], diff_sub_g[l], rel_bias)
        x = _layer_norm(ALPHA * x + mix, ln_g[l, 1], ln_b[l, 1])
        x = _layer_norm(ALPHA * x + 0.5 * _swiglu(x, ffn_w_gate[l, 1], ffn_w_up[l, 1], ffn_w_down[l, 1]),
                        ln_g[l, 2], ln_b[l, 2])
        conv_l.append(c_new)
        k_l.append(k_new)
        v_l.append(v_new)
        s_l.append(s_new)
    return x, jnp.stack(conv_l), jnp.stack(k_l), jnp.stack(v_l), jnp.stack(s_l)


def setup_inputs(seed: int = 0) -> dict:
    key = jax.random.key(seed)
    ks = jax.random.split(key, 24)
    nrm = lambda k, shape: jax.random.normal(k, shape, jnp.float32)
    return {
        'x_prompt': nrm(ks[0], (BATCH, SEQ, D_MODEL)),
        'x_sample': nrm(ks[1], (DEC_BATCH, DEC_SEQ, D_MODEL)),
        'cache_conv': nrm(ks[2], (DEPTH, DEC_BATCH, CONV_WIDTH - 1, D_CONV)) * 0.5,
        'cache_k': nrm(ks[3], (DEPTH, DEC_BATCH, PAST_LEN, N_DIFF, 2 * DIFF_HD)),
        'cache_v': nrm(ks[4], (DEPTH, DEC_BATCH, PAST_LEN, N_DIFF, DIFF_VD)),
        'state_ret': nrm(ks[5], (DEPTH, DEC_BATCH, N_RET, RET_HD, RET_HD)),
        'ln_g': 1.0 + 0.01 * nrm(ks[6], (DEPTH, 3, D_MODEL)),
        'ln_b': 0.01 * nrm(ks[7], (DEPTH, 3, D_MODEL)),
        'ffn_w_gate': nrm(ks[8], (DEPTH, 2, D_MODEL, D_FF)) * D_MODEL ** -0.5,
        'ffn_w_up': nrm(ks[9], (DEPTH, 2, D_MODEL, D_FF)) * D_MODEL ** -0.5,
        'ffn_w_down': nrm(ks[10], (DEPTH, 2, D_FF, D_MODEL)) * (D_FF ** -0.5 * BETA),
        'w_in': nrm(ks[11], (DEPTH, D_MODEL, D_IN)) * D_MODEL ** -0.5,
        'w_out': nrm(ks[12], (DEPTH, D_MIX, D_MODEL)) * (D_MIX ** -0.5 * BETA),
        'conv_w': nrm(ks[13], (DEPTH, CONV_WIDTH, D_CONV)) * CONV_WIDTH ** -0.5,
        'conv_b': 0.01 * nrm(ks[14], (DEPTH, D_CONV)),
        'conv_ln_g': 1.0 + 0.01 * nrm(ks[15], (DEPTH, D_CONV)),
        'conv_ln_b': 0.01 * nrm(ks[16], (DEPTH, D_CONV)),
        'diff_lam': 0.1 * nrm(ks[17], (DEPTH, 4, DIFF_HD)),
        'diff_sub_g': 1.0 + 0.01 * nrm(ks[18], (DEPTH, DIFF_VD)),
        'rel_bias': 0.5 * nrm(ks[19], (N_BUCKETS, N_DIFF)),
    }


def reference(x_prompt, x_sample, cache_conv, cache_k, cache_v, state_ret, ln_g, ln_b, ffn_w_gate, ffn_w_up,
              ffn_w_down, w_in, w_out, conv_w, conv_b, conv_ln_g, conv_ln_b, diff_lam, diff_sub_g, rel_bias):
    y_prompt, conv_p, k_p, v_p, ret_p = _trunk(
        x_prompt, 0, None, None, None, None, ln_g, ln_b, ffn_w_gate, ffn_w_up, ffn_w_down,
        w_in, w_out, conv_w, conv_b, conv_ln_g, conv_ln_b, diff_lam, diff_sub_g, rel_bias)
    y_sample, conv_s, k_s, v_s, ret_s = _trunk(
        x_sample, cache_k.shape[2], cache_conv, cache_k, cache_v, state_ret, ln_g, ln_b, ffn_w_gate, ffn_w_up,
        ffn_w_down, w_in, w_out, conv_w, conv_b, conv_ln_g, conv_ln_b, diff_lam, diff_sub_g, rel_bias)
    return (y_prompt, y_sample, conv_p, k_p, v_p, ret_p, conv_s, k_s, v_s, ret_s)
```

```python
import functools
import math

import jax
import jax.numpy as jnp
from jax import lax
from jax.experimental import pallas as pl
from jax.experimental.pallas import tpu as pltpu

CHUNK = 64
CONV_WIDTH = 31
D_CONV = 512
N_DIFF = 8
DIFF_HD = 64
DIFF_VD = 2 * DIFF_HD
N_RET = 4
RET_HD = 128
N_BUCKETS = 32
MAX_DISTANCE = 128
LN_EPS = 1e-5
ROPE_BASE = 10000.0

D_ATT = N_DIFF * DIFF_VD
D_RET = N_RET * RET_HD
C_CA, C_CG = 0, D_CONV
C_Q = 2 * D_CONV
C_K = C_Q + D_ATT
C_V = C_K + D_ATT
C_RQ = C_V + D_ATT
C_RK = C_RQ + D_RET
C_RV = C_RK + D_RET
C_RG = C_RV + D_RET
D_IN = C_RG + D_RET

MASKED = -1e30
CONV_PAD = 32
V7X_VMEM_LIMIT = 56 * 1024 * 1024

F32 = jnp.float32
BF16 = jnp.bfloat16


def _cparams(sem, vmem=None):
    return pltpu.CompilerParams(dimension_semantics=sem, vmem_limit_bytes=vmem)


def _sigmoid(x):
    return 1.0 / (1.0 + jnp.exp(-x))


def _layer_norm_rows(y, g, b):
    mu = jnp.mean(y, axis=-1, keepdims=True)
    d = y - mu
    var = jnp.mean(d * d, axis=-1, keepdims=True)
    return d * lax.rsqrt(var + LN_EPS) * g + b


def _dot(a, b):
    return jnp.dot(a, b, preferred_element_type=F32)


def _dot_nt(a, b):
    return lax.dot_general(a, b, (((1,), (1,)), ((), ())), preferred_element_type=F32)


def _dot_tn(a, b):
    return lax.dot_general(a, b, (((0,), (0,)), ((), ())), preferred_element_type=F32)


def _ffn_ln_body(alpha, nf, x_ref, wg_ref, wu_ref, wd_ref, g_ref, b_ref, o_ref, xb_ref, acc_ref):
    f = pl.program_id(1)

    @pl.when(f == 0)
    def _():
        xb_ref[...] = x_ref[...].astype(BF16)
        acc_ref[...] = jnp.zeros_like(acc_ref)

    xb = xb_ref[...]
    hg = _dot(xb, wg_ref[...])
    hu = _dot(xb, wu_ref[...])
    act = (hg * _sigmoid(hg) * hu).astype(BF16)
    acc_ref[...] += _dot(act, wd_ref[...])

    @pl.when(f == nf - 1)
    def _():
        y = alpha * x_ref[...] + 0.5 * acc_ref[...]
        o_ref[...] = _layer_norm_rows(y, g_ref[...], b_ref[...])


def _ffn_ln(x, wg, wu, wd, g, b, layer, slot, alpha, tm, tf):
    n, d = x.shape
    ff = wg.shape[-1]
    nf = ff // tf
    return pl.pallas_call(
        functools.partial(_ffn_ln_body, alpha, nf),
        grid=(n // tm, nf),
        in_specs=[
            pl.BlockSpec((tm, d), lambda i, f: (i, 0)),
            pl.BlockSpec((None, None, d, tf), lambda i, f: (layer, slot, 0, f)),
            pl.BlockSpec((None, None, d, tf), lambda i, f: (layer, slot, 0, f)),
            pl.BlockSpec((None, None, tf, d), lambda i, f: (layer, slot, f, 0)),
            pl.BlockSpec((1, d), lambda i, f: (0, 0)),
            pl.BlockSpec((1, d), lambda i, f: (0, 0)),
        ],
        out_specs=pl.BlockSpec((tm, d), lambda i, f: (i, 0)),
        out_shape=jax.ShapeDtypeStruct((n, d), F32),
        scratch_shapes=[pltpu.VMEM((tm, d), BF16), pltpu.VMEM((tm, d), F32)],
        compiler_params=_cparams(("parallel", "arbitrary"), V7X_VMEM_LIMIT),
        name="ffn_ln",
    )(x, wg, wu, wd, g, b)


def _rope_heads(x, cos, sin):
    outs = []
    for h in range(N_RET):
        xh = x[:, h * RET_HD:(h + 1) * RET_HD]
        outs.append(xh * cos + pltpu.roll(xh, RET_HD // 2, 1) * sin)
    return jnp.concatenate(outs, axis=1)


def _in_proj_body(tm, h_ref, w_ref, cos_ref, sin_ref, wk_ref,
                  u_ref, q_ref, kf_ref, kb_ref, vf_ref, vt_ref, rq_ref, rk_ref, rkw_ref, rv_ref, rg_ref):
    hb = h_ref[...].astype(BF16)

    def mm(lo, hi):
        return _dot(hb, w_ref[:, lo:hi])

    u_ref[...] = mm(C_CA, C_CG) * _sigmoid(mm(C_CG, C_Q))
    q_ref[...] = (mm(C_Q, C_K) * (DIFF_HD ** -0.5)).astype(BF16)
    k = mm(C_K, C_V)
    kf_ref[...] = k
    kb_ref[...] = k.astype(BF16)
    v = mm(C_V, C_RQ)
    vf_ref[...] = v
    vt_ref[...] = v.T.reshape(N_DIFF, DIFF_VD, tm).astype(BF16)
    cos = cos_ref[...]
    sin = sin_ref[...]
    rq_ref[...] = _rope_heads(mm(C_RQ, C_RK), cos, sin).astype(BF16)
    rk = _rope_heads(mm(C_RK, C_RV), cos, sin) * (RET_HD ** -0.5)
    rk_ref[...] = rk.astype(BF16)
    rkw_ref[...] = (rk * wk_ref[...]).astype(BF16)
    rv_ref[...] = mm(C_RV, C_RG).astype(BF16)
    rg = mm(C_RG, D_IN)
    rg_ref[...] = rg * _sigmoid(rg)


def _in_proj(h, w_in, layer, cos, sin, wk_tab, bsz, t, tm):
    n, d = h.shape
    nt = t // tm
    row = lambda b, i: (b * nt + i, 0)
    outs = [
        ((n, D_CONV), F32),
        ((n, D_ATT), BF16),
        ((n, D_ATT), F32),
        ((n, D_ATT), BF16),
        ((n, D_ATT), F32),
        None,
        ((n, D_RET), BF16),
        ((n, D_RET), BF16),
        ((n, D_RET), BF16),
        ((n, D_RET), BF16),
        ((n, D_RET), F32),
    ]
    out_shape, out_specs = [], []
    for o in outs:
        if o is None:
            out_shape.append(jax.ShapeDtypeStruct((bsz, N_DIFF, nt, DIFF_VD, tm), BF16))
            out_specs.append(pl.BlockSpec((None, N_DIFF, None, DIFF_VD, tm), lambda b, i: (b, 0, i, 0, 0)))
        else:
            out_shape.append(jax.ShapeDtypeStruct(*o))
            out_specs.append(pl.BlockSpec((tm, o[0][1]), row))
    return pl.pallas_call(
        functools.partial(_in_proj_body, tm),
        grid=(bsz, nt),
        in_specs=[
            pl.BlockSpec((tm, d), row),
            pl.BlockSpec((None, d, D_IN), lambda b, i: (layer, 0, 0), pipeline_mode=pl.Buffered(1)),
            pl.BlockSpec((tm, RET_HD), lambda b, i: (i, 0)),
            pl.BlockSpec((tm, RET_HD), lambda b, i: (i, 0)),
            pl.BlockSpec((tm, D_RET), lambda b, i: (0, 0)),
        ],
        out_specs=out_specs,
        out_shape=out_shape,
        compiler_params=_cparams(("parallel", "parallel"), V7X_VMEM_LIMIT),
        name="in_proj",
    )(h, w_in, cos, sin, wk_tab)


def _conv_body(tm, nt, u_ref, prev_ref, w_ref, cb_ref, g_ref, b_ref, o_ref, new_ref, ext_ref):
    i = pl.program_id(1)
    hist = CONV_WIDTH - 1

    @pl.when(i == 0)
    def _():
        ext_ref[0:CONV_PAD - hist, :] = jnp.zeros((CONV_PAD - hist, D_CONV), F32)
        ext_ref[CONV_PAD - hist:CONV_PAD, :] = prev_ref[...]

    @pl.when(i > 0)
    def _():
        ext_ref[0:CONV_PAD, :] = ext_ref[tm:tm + CONV_PAD, :]

    ext_ref[CONV_PAD:CONV_PAD + tm, :] = u_ref[...]

    rb = 32
    cb = cb_ref[...]
    g = g_ref[...]
    b = b_ref[...]
    for r0 in range(0, tm, rb):
        acc = jnp.zeros((rb, D_CONV), F32)
        for w in range(CONV_WIDTH):
            start = CONV_PAD - hist + w + r0
            acc = acc + ext_ref[start:start + rb, :] * w_ref[w:w + 1, :]
        y = _layer_norm_rows(acc + cb, g, b)
        o_ref[r0:r0 + rb, :] = (y * _sigmoid(y)).astype(o_ref.dtype)

    @pl.when(i == nt - 1)
    def _():
        new_ref[...] = ext_ref[CONV_PAD + tm - hist:CONV_PAD + tm, :]


def _conv_module(u, prev, conv_w, conv_b, g, b, layer, bsz, t, tm):
    n = u.shape[0]
    nt = t // tm
    assert t >= CONV_PAD and tm >= CONV_PAD and tm % 32 == 0
    vec = lambda: pl.BlockSpec((None, 1, D_CONV), lambda bb, i: (layer, 0, 0))
    return pl.pallas_call(
        functools.partial(_conv_body, tm, nt),
        grid=(bsz, nt),
        in_specs=[
            pl.BlockSpec((tm, D_CONV), lambda bb, i: (bb * nt + i, 0)),
            pl.BlockSpec((None, CONV_WIDTH - 1, D_CONV), lambda bb, i: (bb, 0, 0)),
            pl.BlockSpec((None, CONV_WIDTH, D_CONV), lambda bb, i: (layer, 0, 0)),
            vec(), vec(), vec(),
        ],
        out_specs=[
            pl.BlockSpec((tm, D_CONV), lambda bb, i: (bb * nt + i, 0)),
            pl.BlockSpec((None, CONV_WIDTH - 1, D_CONV), lambda bb, i: (bb, 0, 0)),
        ],
        out_shape=[
            jax.ShapeDtypeStruct((n, D_CONV), BF16),
            jax.ShapeDtypeStruct((bsz, CONV_WIDTH - 1, D_CONV), F32),
        ],
        scratch_shapes=[pltpu.VMEM((tm + CONV_PAD, D_CONV), F32)],
        compiler_params=_cparams(("parallel", "arbitrary")),
        name="conv_module",
    )(u, prev, conv_w, conv_b.reshape(-1, 1, D_CONV), g.reshape(-1, 1, D_CONV), b.reshape(-1, 1, D_CONV))


def _rel_bucket(rel):
    half = N_BUCKETS // 2
    max_exact = half // 2
    ret = jnp.where(rel > 0, half, 0)
    n = jnp.abs(rel)
    nf = jnp.maximum(n, 1).astype(F32)
    large = max_exact + (jnp.log(nf / max_exact) / math.log(MAX_DISTANCE / max_exact)
                         * (half - max_exact)).astype(jnp.int32)
    large = jnp.minimum(large, half - 1)
    return ret + jnp.where(n < max_exact, n, large)


def _lam_value(lam_ref, lam_init):
    lf = lam_ref[...]
    a = jnp.sum(lf[0:1, :] * lf[1:2, :], axis=1, keepdims=True)
    b = jnp.sum(lf[2:3, :] * lf[3:4, :], axis=1, keepdims=True)
    return jnp.exp(a) - jnp.exp(b) + lam_init


def _block_diag_q(q):
    lane = lax.broadcasted_iota(jnp.int32, q.shape, 1)
    zero = jnp.zeros_like(q)
    return jnp.concatenate([jnp.where(lane < DIFF_HD, q, zero), jnp.where(lane >= DIFF_HD, q, zero)], axis=0)


def _attn_prompt_body(t, lam_init, q_ref, k_ref, vt_ref, bias_ref, lam_ref, subg_ref, o_ref,
                      qbd_ref, m_ref, l_ref, acc_ref):
    i = pl.program_id(2)
    qbd_ref[...] = _block_diag_q(q_ref[...])
    m_ref[...] = jnp.full(m_ref.shape, MASKED, F32)
    l_ref[...] = jnp.zeros_like(l_ref)
    acc_ref[...] = jnp.zeros_like(acc_ref)

    def step(j, bias):
        k = k_ref[pl.ds(pl.multiple_of(j * t, t), t), :]
        s = _dot_nt(k, qbd_ref[...])
        if bias is not None:
            s = s + bias
        m_old = m_ref[...]
        m_new = jnp.maximum(m_old, jnp.max(s, axis=0, keepdims=True))
        alpha = jnp.exp(m_old - m_new)
        p = jnp.exp(s - m_new)
        l_ref[...] = alpha * l_ref[...] + jnp.sum(p, axis=0, keepdims=True)
        acc_ref[...] = alpha * acc_ref[...] + _dot(vt_ref[j], p.astype(BF16))
        m_ref[...] = m_new

    step(i, bias_ref[0])

    @pl.when(i >= 1)
    def _():
        step(i - 1, bias_ref[1])

    def far(j, carry):
        step(j, None)
        return carry

    lax.fori_loop(0, i - 1, far, 0)

    o = acc_ref[...] / l_ref[...]
    lam = _lam_value(lam_ref, lam_init)
    a = o[:, :t] - lam * o[:, t:]
    a = a * lax.rsqrt(jnp.mean(a * a, axis=0, keepdims=True) + LN_EPS)
    o_ref[...] = (a.T * subg_ref[...] * (1.0 - lam_init)).astype(o_ref.dtype)


def _attn_prompt(q, kb, vt, rel_bias, diff_lam, sub_g, layer, lam_init, bsz, seq, t):
    assert t >= MAX_DISTANCE and t % CHUNK == 0 and seq % t == 0
    n = q.shape[0]
    nq = seq // t
    r = jnp.arange(t, dtype=jnp.int32)[:, None]
    c = jnp.arange(t, dtype=jnp.int32)[None, :]
    far_bias = rel_bias[_rel_bucket(jnp.full((), -MAX_DISTANCE, jnp.int32))]
    b0 = rel_bias[_rel_bucket(r - c)] - far_bias
    b0 = jnp.where(((r // CHUNK) <= (c // CHUNK))[:, :, None], b0, MASKED)
    b1 = rel_bias[_rel_bucket(r - c - t)] - far_bias
    tiles = jnp.transpose(jnp.stack([b0, b1]), (3, 0, 1, 2)).astype(F32)
    tiles = jnp.concatenate([tiles, tiles], axis=-1)

    return pl.pallas_call(
        functools.partial(_attn_prompt_body, t, lam_init),
        grid=(bsz, N_DIFF, nq),
        in_specs=[
            pl.BlockSpec((t, DIFF_VD), lambda b, h, i: (b * nq + i, h)),
            pl.BlockSpec((seq, DIFF_VD), lambda b, h, i: (b, h)),
            pl.BlockSpec((None, None, nq, DIFF_VD, t), lambda b, h, i: (b, h, 0, 0, 0)),
            pl.BlockSpec((None, 2, t, 2 * t), lambda b, h, i: (h, 0, 0, 0)),
            pl.BlockSpec((None, 4, DIFF_HD), lambda b, h, i: (layer, 0, 0)),
            pl.BlockSpec((None, 1, DIFF_VD), lambda b, h, i: (layer, 0, 0)),
        ],
        out_specs=pl.BlockSpec((t, DIFF_VD), lambda b, h, i: (b * nq + i, h)),
        out_shape=jax.ShapeDtypeStruct((n, D_ATT), BF16),
        scratch_shapes=[
            pltpu.VMEM((2 * t, DIFF_VD), BF16),
            pltpu.VMEM((1, 2 * t), F32),
            pltpu.VMEM((1, 2 * t), F32),
            pltpu.VMEM((DIFF_VD, 2 * t), F32),
        ],
        compiler_params=_cparams(("parallel", "parallel", "arbitrary"), V7X_VMEM_LIMIT),
        name="attn_prompt",
    )(q, kb, vt, tiles, diff_lam, sub_g.reshape(-1, 1, DIFF_VD))


def _attn_sample_body(tq, lam_init, q_ref, kp_ref, kn_ref, vp_ref, vn_ref, bp_ref, bn_ref, lam_ref, subg_ref, o_ref):
    qbd = _block_diag_q(q_ref[...])
    s_p = _dot_nt(qbd, kp_ref[...].astype(BF16)) + bp_ref[...]
    s_n = _dot_nt(qbd, kn_ref[...]) + bn_ref[...]
    m = jnp.maximum(jnp.max(s_p, axis=1, keepdims=True), jnp.max(s_n, axis=1, keepdims=True))
    p_p = jnp.exp(s_p - m)
    p_n = jnp.exp(s_n - m)
    l = jnp.sum(p_p, axis=1, keepdims=True) + jnp.sum(p_n, axis=1, keepdims=True)
    o = (_dot(p_p.astype(BF16), vp_ref[...].astype(BF16)) + _dot(p_n.astype(BF16), vn_ref[...].astype(BF16))) / l
    lam = _lam_value(lam_ref, lam_init)
    a = o[:tq, :] - lam * o[tq:, :]
    a = a * lax.rsqrt(jnp.mean(a * a, axis=1, keepdims=True) + LN_EPS)
    o_ref[...] = (a * subg_ref[...] * (1.0 - lam_init)).astype(o_ref.dtype)


def _attn_sample(q, kb, vf, cache_k, cache_v, rel_bias, diff_lam, sub_g, layer, lam_init, bsz, tq):
    assert tq <= 128
    n = q.shape[0]
    past = cache_k.shape[2]
    qpos = past + jnp.arange(tq, dtype=jnp.int32)[:, None]
    kpos = jnp.arange(past + tq, dtype=jnp.int32)[None, :]
    bias = rel_bias[_rel_bucket(kpos - qpos)]
    bias = jnp.where(((kpos // CHUNK) <= (qpos // CHUNK))[:, :, None], bias, MASKED)
    bias = jnp.transpose(bias, (2, 0, 1)).astype(F32)
    bias = jnp.concatenate([bias, bias], axis=1)
    cache_k = cache_k.reshape(cache_k.shape[:3] + (D_ATT,))
    cache_v = cache_v.reshape(cache_v.shape[:3] + (D_ATT,))
    cache_spec = pl.BlockSpec((None, None, past, DIFF_VD), lambda b, h: (layer, b, 0, h))
    new_spec = pl.BlockSpec((tq, DIFF_VD), lambda b, h: (b, h))
    return pl.pallas_call(
        functools.partial(_attn_sample_body, tq, lam_init),
        grid=(bsz, N_DIFF),
        in_specs=[
            new_spec, cache_spec, new_spec, cache_spec, new_spec,
            pl.BlockSpec((None, 2 * tq, past), lambda b, h: (h, 0, 0)),
            pl.BlockSpec((None, 2 * tq, tq), lambda b, h: (h, 0, 0)),
            pl.BlockSpec((None, 4, DIFF_HD), lambda b, h: (layer, 0, 0)),
            pl.BlockSpec((None, 1, DIFF_VD), lambda b, h: (layer, 0, 0)),
        ],
        out_specs=new_spec,
        out_shape=jax.ShapeDtypeStruct((n, D_ATT), BF16),
        compiler_params=_cparams(("parallel", "parallel")),
        name="attn_sample",
    )(q, cache_k, kb, cache_v, vf, bias[:, :, :past], bias[:, :, past:], diff_lam, sub_g.reshape(-1, 1, DIFF_VD))


def _retention_body(nc, q_ref, k_ref, kw_ref, v_ref, gate_ref, s0_ref, decay_ref, wq_ref, gl_ref,
                    o_ref, snew_ref, s_ref):
    c = pl.program_id(2)

    @pl.when(c == 0)
    def _():
        s_ref[...] = s0_ref[...]

    q = q_ref[...]
    v = v_ref[...]
    s_prev = s_ref[...]
    inner = _dot_nt(q, k_ref[...]) * decay_ref[...]
    o = _dot(inner.astype(BF16), v) + _dot(q, s_prev.astype(BF16)) * wq_ref[...]
    s_new = gl_ref[...] * s_prev + _dot_tn(kw_ref[...], v)
    s_ref[...] = s_new

    @pl.when(c == nc - 1)
    def _():
        snew_ref[...] = s_new

    mu = jnp.mean(o, axis=1, keepdims=True)
    d = o - mu
    var = jnp.mean(d * d, axis=1, keepdims=True)
    o_ref[...] = (gate_ref[...] * (d * lax.rsqrt(var + LN_EPS))).astype(o_ref.dtype)


def _retention_tables(lc):
    log_g = jnp.log1p(-jnp.exp2(-5.0 - jnp.arange(N_RET, dtype=F32)))
    idx = jnp.arange(lc, dtype=F32)
    rel = idx[:, None] - idx[None, :]
    decay = jnp.where(rel >= 0, jnp.exp(jnp.maximum(rel, 0.0)[None] * log_g[:, None, None]), 0.0)
    w_k = jnp.exp((lc - 1.0 - idx)[None, :] * log_g[:, None])
    w_q = jnp.exp((idx + 1.0)[None, :] * log_g[:, None])
    g_l = jnp.exp(lc * log_g)
    widen = lambda w: jnp.repeat(w.T, RET_HD, axis=1)
    gl_tab = jnp.broadcast_to(g_l[:, None, None], (N_RET, 1, RET_HD))
    return decay, widen(w_q), widen(w_k), gl_tab


def _retention(rq, rk, rkw, rv, gate, s0, decay, wq_tab, gl_tab, bsz, t, lc):
    n = rq.shape[0]
    nc = t // lc
    tile = pl.BlockSpec((lc, RET_HD), lambda b, h, c: (b * nc + c, h))
    state = pl.BlockSpec((None, None, RET_HD, RET_HD), lambda b, h, c: (b, h, 0, 0))
    return pl.pallas_call(
        functools.partial(_retention_body, nc),
        grid=(bsz, N_RET, nc),
        in_specs=[
            tile, tile, tile, tile, tile, state,
            pl.BlockSpec((None, lc, lc), lambda b, h, c: (h, 0, 0)),
            pl.BlockSpec((lc, RET_HD), lambda b, h, c: (0, h)),
            pl.BlockSpec((None, 1, RET_HD), lambda b, h, c: (h, 0, 0)),
        ],
        out_specs=[tile, state],
        out_shape=[
            jax.ShapeDtypeStruct((n, D_RET), BF16),
            jax.ShapeDtypeStruct((bsz, N_RET, RET_HD, RET_HD), F32),
        ],
        scratch_shapes=[pltpu.VMEM((RET_HD, RET_HD), F32)],
        compiler_params=_cparams(("parallel", "parallel", "arbitrary")),
        name="retention",
    )(rq, rk, rkw, rv, gate, s0, decay, wq_tab, gl_tab)


def _out_proj_ln_body(alpha, x_ref, c_ref, a_ref, r_ref, w_ref, g_ref, b_ref, o_ref):
    e0 = D_CONV
    e1 = D_CONV + D_ATT
    mix = (_dot(c_ref[...], w_ref[0:e0, :]) + _dot(a_ref[...], w_ref[e0:e1, :])
           + _dot(r_ref[...], w_ref[e1:e1 + D_RET, :]))
    o_ref[...] = _layer_norm_rows(alpha * x_ref[...] + mix, g_ref[...], b_ref[...])


def _out_proj_ln(x, conv_out, attn_out, ret_out, w_out, g, b, layer, alpha, tm):
    n, d = x.shape
    d_mix = D_CONV + D_ATT + D_RET
    tile = lambda w: pl.BlockSpec((tm, w), lambda i: (i, 0))
    return pl.pallas_call(
        functools.partial(_out_proj_ln_body, alpha),
        grid=(n // tm,),
        in_specs=[
            tile(d), tile(D_CONV), tile(D_ATT), tile(D_RET),
            pl.BlockSpec((None, d_mix, d), lambda i: (layer, 0, 0), pipeline_mode=pl.Buffered(1)),
            pl.BlockSpec((1, d), lambda i: (0, 0)),
            pl.BlockSpec((1, d), lambda i: (0, 0)),
        ],
        out_specs=tile(d),
        out_shape=jax.ShapeDtypeStruct((n, d), F32),
        compiler_params=_cparams(("parallel",), V7X_VMEM_LIMIT),
        name="out_proj_ln",
    )(x, conv_out, attn_out, ret_out, w_out, g, b)


def _rope_tables(pos0, t):
    half = RET_HD // 2
    inv = 1.0 / (ROPE_BASE ** (jnp.arange(half, dtype=F32) / half))
    ang = (pos0 + jnp.arange(t, dtype=jnp.int32)).astype(F32)[:, None] * inv[None, :]
    cos = jnp.cos(ang)
    sin = jnp.sin(ang)
    return jnp.concatenate([cos, cos], axis=1), jnp.concatenate([-sin, sin], axis=1)


def _trunk(x, pos0, caches, params, tiles):
    (ln_g, ln_b, wg, wu, wd, w_in, w_out, conv_w, conv_b, conv_ln_g, conv_ln_b, diff_lam, diff_sub_g,
     rel_bias) = params
    bsz, t, d = x.shape
    depth = ln_g.shape[0]
    alpha = (2 * depth) ** 0.25
    tm_ffn, tf, tm_tok, tm_out = tiles
    prompt = caches is None
    lc = tm_tok
    cos, sin = _rope_tables(pos0, t)
    decay, wq_tab, wk_tab, gl_tab = _retention_tables(lc)
    x = x.reshape(bsz * t, d)
    conv_l, k_l, v_l, s_l = [], [], [], []
    for l in range(depth):
        lam_init = 0.8 - 0.6 * math.exp(-0.3 * l)
        if prompt:
            conv_prev = jnp.zeros((bsz, CONV_WIDTH - 1, D_CONV), F32)
            ret_prev = jnp.zeros((bsz, N_RET, RET_HD, RET_HD), F32)
        else:
            cache_conv, cache_k, cache_v, state_ret = caches
            conv_prev, ret_prev = cache_conv[l], state_ret[l]
        x = _ffn_ln(x, wg, wu, wd, ln_g[l, 0:1], ln_b[l, 0:1], l, 0, alpha, tm_ffn, tf)
        u, q, kf, kb, vf, vt, rq, rk, rkw, rv, gate = _in_proj(x, w_in, l, cos, sin, wk_tab, bsz, t, tm_tok)
        conv_out, conv_new = _conv_module(u, conv_prev, conv_w, conv_b, conv_ln_g, conv_ln_b, l, bsz, t, tm_tok)
        if prompt:
            attn_out = _attn_prompt(q, kb, vt, rel_bias, diff_lam, diff_sub_g, l, lam_init, bsz, t, tm_tok)
        else:
            attn_out = _attn_sample(q, kb, vf, cache_k, cache_v, rel_bias, diff_lam, diff_sub_g, l, lam_init, bsz, t)
        ret_out, s_new = _retention(rq, rk, rkw, rv, gate, ret_prev, decay, wq_tab, gl_tab, bsz, t, lc)
        x = _out_proj_ln(x, conv_out, attn_out, ret_out, w_out, ln_g[l, 1:2], ln_b[l, 1:2], l, alpha, tm_out)
        x = _ffn_ln(x, wg, wu, wd, ln_g[l, 2:3], ln_b[l, 2:3], l, 1, alpha, tm_ffn, tf)
        conv_l.append(conv_new)
        k_l.append(kf.reshape(bsz, t, N_DIFF, DIFF_VD))
        v_l.append(vf.reshape(bsz, t, N_DIFF, DIFF_VD))
        s_l.append(s_new)
    return x.reshape(bsz, t, d), jnp.stack(conv_l), jnp.stack(k_l), jnp.stack(v_l), jnp.stack(s_l)


def _pick(n, cap):
    if n <= cap:
        return n
    best = None
    for c in range(128, cap + 1, 128):
        if n % c == 0:
            best = c
    assert best is not None
    return best


def kernel(x_prompt, x_sample, cache_conv, cache_k, cache_v, state_ret, ln_g, ln_b, ffn_w_gate, ffn_w_up,
           ffn_w_down, w_in, w_out, conv_w, conv_b, conv_ln_g, conv_ln_b, diff_lam, diff_sub_g, rel_bias):
    params = (ln_g, ln_b, ffn_w_gate.astype(BF16), ffn_w_up.astype(BF16), ffn_w_down.astype(BF16),
              w_in.astype(BF16), w_out.astype(BF16), conv_w, conv_b, conv_ln_g, conv_ln_b, diff_lam, diff_sub_g,
              rel_bias)
    tf = _pick(ffn_w_gate.shape[-1], 512)
    bp, tp, _ = x_prompt.shape
    bs, ts, _ = x_sample.shape
    tok_p = _pick(tp, 256)
    y_p, conv_p, k_p, v_p, ret_p = _trunk(
        x_prompt, 0, None, params, (_pick(bp * tp, 512), tf, tok_p, _pick(bp * tp, 512)))
    y_s, conv_s, k_s, v_s, ret_s = _trunk(
        x_sample, cache_k.shape[2], (cache_conv, cache_k, cache_v, state_ret), params,
        (_pick(bs * ts, 512), tf, ts, _pick(bs * ts, 512)))
    return (y_p, y_s, conv_p, k_p, v_p, ret_p, conv_s, k_s, v_s, ret_s)
```

```python
import functools
import math

import jax
import jax.numpy as jnp
import numpy as np
from jax import lax
from jax.experimental import pallas as pl
from jax.experimental.pallas import tpu as pltpu

CHUNK = 64
CONV_WIDTH = 31
D_CONV = 512
N_DIFF = 8
DIFF_HD = 64
DIFF_VD = 2 * DIFF_HD
N_RET = 4
RET_HD = 128
N_BUCKETS = 32
MAX_DISTANCE = 128
LN_EPS = 1e-5
ROPE_BASE = 10000.0

D_ATT = N_DIFF * DIFF_VD
VT_ROWS = DIFF_VD + 16
D_RET = N_RET * RET_HD
C_CA, C_CG = 0, D_CONV
C_Q = 2 * D_CONV
C_K = C_Q + D_ATT
C_V = C_K + D_ATT
C_RQ = C_V + D_ATT
C_RK = C_RQ + D_RET
C_RV = C_RK + D_RET
C_RG = C_RV + D_RET
D_IN = C_RG + D_RET

LOG2E = math.log2(math.e)
MASKED = -1e30
ATTN_ROW_BLOCK = 32
CONV_PAD = 32
V7X_VMEM_LIMIT = 56 * 1024 * 1024

F32 = jnp.float32
BF16 = jnp.bfloat16


def _cparams(sem, vmem=None):
    return pltpu.CompilerParams(dimension_semantics=sem, vmem_limit_bytes=vmem)


def _sigmoid(x):
    return 1.0 / (1.0 + jnp.exp(-x))


def _layer_norm_rows(y, g, b):
    mu = jnp.mean(y, axis=-1, keepdims=True)
    d = y - mu
    var = jnp.mean(d * d, axis=-1, keepdims=True)
    return d * lax.rsqrt(var + LN_EPS) * g + b


def _dot(a, b):
    return jnp.dot(a, b, preferred_element_type=F32)


def _dot_nt(a, b):
    return lax.dot_general(a, b, (((1,), (1,)), ((), ())), preferred_element_type=F32)


def _dot_tn(a, b):
    return lax.dot_general(a, b, (((0,), (0,)), ((), ())), preferred_element_type=F32)


def _ffn_ln_body(alpha, nf, x_ref, wg_ref, wu_ref, wd_ref, g_ref, b_ref, o_ref, xb_ref, acc_ref):
    f = pl.program_id(1)

    @pl.when(f == 0)
    def _():
        xb_ref[...] = x_ref[...].astype(BF16)
        acc_ref[...] = jnp.zeros_like(acc_ref)

    xb = xb_ref[...]
    hg = _dot(xb, wg_ref[...])
    hu = _dot(xb, wu_ref[...])
    act = (hg * _sigmoid(hg) * hu).astype(BF16)
    acc_ref[...] += _dot(act, wd_ref[...])

    @pl.when(f == nf - 1)
    def _():
        y = alpha * x_ref[...] + 0.5 * acc_ref[...]
        o_ref[...] = _layer_norm_rows(y, g_ref[...], b_ref[...])


def _ffn_ln(x, wg, wu, wd, g, b, layer, slot, alpha, tm, tf):
    n, d = x.shape
    ff = wg.shape[-1]
    nf = ff // tf
    return pl.pallas_call(
        functools.partial(_ffn_ln_body, alpha, nf),
        grid=(n // tm, nf),
        in_specs=[
            pl.BlockSpec((tm, d), lambda i, f: (i, 0)),
            pl.BlockSpec((None, None, d, tf), lambda i, f: (layer, slot, 0, f)),
            pl.BlockSpec((None, None, d, tf), lambda i, f: (layer, slot, 0, f)),
            pl.BlockSpec((None, None, tf, d), lambda i, f: (layer, slot, f, 0)),
            pl.BlockSpec((1, d), lambda i, f: (0, 0)),
            pl.BlockSpec((1, d), lambda i, f: (0, 0)),
        ],
        out_specs=pl.BlockSpec((tm, d), lambda i, f: (i, 0)),
        out_shape=jax.ShapeDtypeStruct((n, d), F32),
        scratch_shapes=[pltpu.VMEM((tm, d), BF16), pltpu.VMEM((tm, d), F32)],
        compiler_params=_cparams(("parallel", "arbitrary"), V7X_VMEM_LIMIT),
        name="ffn_ln",
    )(x, wg, wu, wd, g, b)


def _rope_heads(x, cos, sin):
    outs = []
    for h in range(N_RET):
        xh = x[:, h * RET_HD:(h + 1) * RET_HD]
        outs.append(xh * cos + pltpu.roll(xh, RET_HD // 2, 1) * sin)
    return jnp.concatenate(outs, axis=1)


def _in_proj_body(tm, h_ref, w_ref, cos_ref, sin_ref, wk_ref,
                  u_ref, q_ref, kf_ref, kb_ref, vf_ref, vt_ref, rq_ref, rk_ref, rkw_ref, rv_ref, rg_ref):
    hb = h_ref[...].astype(BF16)

    def mm(lo, hi):
        return _dot(hb, w_ref[:, lo:hi])

    u_ref[...] = mm(C_CA, C_CG) * _sigmoid(mm(C_CG, C_Q))
    q_ref[...] = (mm(C_Q, C_K) * (DIFF_HD ** -0.5 * LOG2E)).astype(BF16)
    k = mm(C_K, C_V)
    kf_ref[...] = k
    kb_ref[...] = k.astype(BF16)
    v = mm(C_V, C_RQ)
    vf_ref[...] = v
    vt_ref[:, 0:DIFF_VD, :] = v.T.reshape(N_DIFF, DIFF_VD, tm).astype(BF16)
    pad_row = lax.broadcasted_iota(jnp.int32, (N_DIFF, VT_ROWS - DIFF_VD, tm), 1)
    vt_ref[:, DIFF_VD:VT_ROWS, :] = jnp.where(pad_row == 0, 1.0, 0.0).astype(BF16)
    cos = cos_ref[...]
    sin = sin_ref[...]
    rq_ref[...] = _rope_heads(mm(C_RQ, C_RK), cos, sin).astype(BF16)
    rk = _rope_heads(mm(C_RK, C_RV), cos, sin) * (RET_HD ** -0.5)
    rk_ref[...] = rk.astype(BF16)
    rkw_ref[...] = (rk * wk_ref[...]).astype(BF16)
    rv_ref[...] = mm(C_RV, C_RG).astype(BF16)
    rg = mm(C_RG, D_IN)
    rg_ref[...] = rg * _sigmoid(rg)


def _in_proj(h, w_in, layer, cos, sin, wk_tab, bsz, t, tm):
    n, d = h.shape
    nt = t // tm
    row = lambda b, i: (b * nt + i, 0)
    outs = [
        ((n, D_CONV), F32),
        ((n, D_ATT), BF16),
        ((n, D_ATT), F32),
        ((n, D_ATT), BF16),
        ((n, D_ATT), F32),
        None,
        ((n, D_RET), BF16),
        ((n, D_RET), BF16),
        ((n, D_RET), BF16),
        ((n, D_RET), BF16),
        ((n, D_RET), F32),
    ]
    out_shape, out_specs = [], []
    for o in outs:
        if o is None:
            out_shape.append(jax.ShapeDtypeStruct((bsz, N_DIFF, nt, VT_ROWS, tm), BF16))
            out_specs.append(pl.BlockSpec((None, N_DIFF, None, VT_ROWS, tm), lambda b, i: (b, 0, i, 0, 0)))
        else:
            out_shape.append(jax.ShapeDtypeStruct(*o))
            out_specs.append(pl.BlockSpec((tm, o[0][1]), row))
    return pl.pallas_call(
        functools.partial(_in_proj_body, tm),
        grid=(bsz, nt),
        in_specs=[
            pl.BlockSpec((tm, d), row),
            pl.BlockSpec((None, d, D_IN), lambda b, i: (layer, 0, 0), pipeline_mode=pl.Buffered(1)),
            pl.BlockSpec((tm, RET_HD), lambda b, i: (i, 0)),
            pl.BlockSpec((tm, RET_HD), lambda b, i: (i, 0)),
            pl.BlockSpec((tm, D_RET), lambda b, i: (0, 0)),
        ],
        out_specs=out_specs,
        out_shape=out_shape,
        compiler_params=_cparams(("parallel", "parallel"), V7X_VMEM_LIMIT),
        name="in_proj",
    )(h, w_in, cos, sin, wk_tab)


def _conv_body(tm, nt, u_ref, prev_ref, w_ref, cb_ref, g_ref, b_ref, o_ref, new_ref, ext_ref):
    i = pl.program_id(1)
    hist = CONV_WIDTH - 1

    @pl.when(i == 0)
    def _():
        ext_ref[0:CONV_PAD - hist, :] = jnp.zeros((CONV_PAD - hist, D_CONV), F32)
        ext_ref[CONV_PAD - hist:CONV_PAD, :] = prev_ref[...]

    @pl.when(i > 0)
    def _():
        ext_ref[0:CONV_PAD, :] = ext_ref[tm:tm + CONV_PAD, :]

    ext_ref[CONV_PAD:CONV_PAD + tm, :] = u_ref[...]

    rb = 32
    cb = cb_ref[...]
    g = g_ref[...]
    b = b_ref[...]
    for r0 in range(0, tm, rb):
        acc = jnp.zeros((rb, D_CONV), F32)
        for w in range(CONV_WIDTH):
            start = CONV_PAD - hist + w + r0
            acc = acc + ext_ref[start:start + rb, :] * w_ref[w:w + 1, :]
        y = _layer_norm_rows(acc + cb, g, b)
        o_ref[r0:r0 + rb, :] = (y * _sigmoid(y)).astype(o_ref.dtype)

    @pl.when(i == nt - 1)
    def _():
        new_ref[...] = ext_ref[CONV_PAD + tm - hist:CONV_PAD + tm, :]


def _conv_module(u, prev, conv_w, conv_b, g, b, layer, bsz, t, tm):
    n = u.shape[0]
    nt = t // tm
    assert t >= CONV_PAD and tm >= CONV_PAD and tm % 32 == 0
    vec = lambda: pl.BlockSpec((None, 1, D_CONV), lambda bb, i: (layer, 0, 0))
    return pl.pallas_call(
        functools.partial(_conv_body, tm, nt),
        grid=(bsz, nt),
        in_specs=[
            pl.BlockSpec((tm, D_CONV), lambda bb, i: (bb * nt + i, 0)),
            pl.BlockSpec((None, CONV_WIDTH - 1, D_CONV), lambda bb, i: (bb, 0, 0)),
            pl.BlockSpec((None, CONV_WIDTH, D_CONV), lambda bb, i: (layer, 0, 0)),
            vec(), vec(), vec(),
        ],
        out_specs=[
            pl.BlockSpec((tm, D_CONV), lambda bb, i: (bb * nt + i, 0)),
            pl.BlockSpec((None, CONV_WIDTH - 1, D_CONV), lambda bb, i: (bb, 0, 0)),
        ],
        out_shape=[
            jax.ShapeDtypeStruct((n, D_CONV), BF16),
            jax.ShapeDtypeStruct((bsz, CONV_WIDTH - 1, D_CONV), F32),
        ],
        scratch_shapes=[pltpu.VMEM((tm + CONV_PAD, D_CONV), F32)],
        compiler_params=_cparams(("parallel", "arbitrary")),
        name="conv_module",
    )(u, prev, conv_w, conv_b.reshape(-1, 1, D_CONV), g.reshape(-1, 1, D_CONV), b.reshape(-1, 1, D_CONV))


def _rel_bucket(rel):
    half = N_BUCKETS // 2
    max_exact = half // 2
    ret = jnp.where(rel > 0, half, 0)
    n = jnp.abs(rel)
    nf = jnp.maximum(n, 1).astype(F32)
    large = max_exact + (jnp.log(nf / max_exact) / math.log(MAX_DISTANCE / max_exact)
                         * (half - max_exact)).astype(jnp.int32)
    large = jnp.minimum(large, half - 1)
    return ret + jnp.where(n < max_exact, n, large)


def _toeplitz_bias(rel_bias, rel_of_diff, rows, cols):
    period = rows + cols
    x = np.arange(period)
    diff = np.where(x < cols, x, x - period)
    rel = jnp.asarray(rel_of_diff(diff), jnp.int32)
    wp = rel_bias[_rel_bucket(rel)].astype(F32).T
    flat = jnp.tile(wp, (1, rows))[:, :rows * (period - 1)]
    return flat.reshape(-1, rows, period - 1)[:, :, :cols]


def _lam_value(lam_ref, lam_init):
    lf = lam_ref[...]
    a = jnp.sum(lf[0:1, :] * lf[1:2, :], axis=1, keepdims=True)
    b = jnp.sum(lf[2:3, :] * lf[3:4, :], axis=1, keepdims=True)
    return jnp.exp(a) - jnp.exp(b) + lam_init


def _block_diag_q(q):
    lane = lax.broadcasted_iota(jnp.int32, q.shape, 1)
    zero = jnp.zeros_like(q)
    return jnp.concatenate([jnp.where(lane < DIFF_HD, q, zero), jnp.where(lane >= DIFF_HD, q, zero)], axis=0)


def _attn_prompt_body(t, tv, nq, lam_init, q_ref, k_ref, vt_ref, bias_ref, lam_ref, subg_ref, o_ref,
                      qbd_ref, sa_ref, sb_ref, mxa_ref, mxb_ref, pa_ref, pb_ref, m_ref, acc_ref):
    i = pl.program_id(2)
    sub = t // tv
    rb = ATTN_ROW_BLOCK
    lanes = 2 * t
    qbd_ref[...] = _block_diag_q(q_ref[...])
    m_ref[...] = jnp.full(m_ref.shape, MASKED, F32)
    acc_ref[...] = jnp.zeros_like(acc_ref)
    pa_ref[...] = jnp.zeros_like(pa_ref)

    def scores(j, dst):
        s_dst, mx_dst = dst
        k = k_ref[pl.ds(pl.multiple_of(j * t, t), t), :]
        s = _dot_nt(k, qbd_ref[...])
        s_dst[...] = s
        mx_dst[...] = jnp.max(s.reshape(t // 8, 8, lanes), axis=0)

    def p_times_v(j, p_ref):
        out = None
        for u in range(sub):
            part = _dot(vt_ref[j * sub + u], p_ref[u * tv:(u + 1) * tv, :])
            out = part if out is None else out + part
        return out

    def step(n, bias_idx, src, dst, p_src, p_dst):
        s_src, mx_src = src
        j = i - n
        j_next = jnp.maximum(j - 1, 0)
        if dst is not src:
            scores(j_next, dst)

        def rows(c):
            sc = s_src[c * rb:(c + 1) * rb, :]
            if bias_idx is not None:
                sc = sc + bias_ref[bias_idx, c * rb:(c + 1) * rb, :]
            return sc

        if bias_idx is None:
            mx = mx_src[...]
        else:
            mx = None
            for c in range(t // rb):
                cm = jnp.max(rows(c).reshape(rb // 8, 8, lanes), axis=0)
                mx = cm if mx is None else jnp.maximum(mx, cm)
        m_old = m_ref[...]
        m_new = jnp.maximum(m_old, jnp.max(mx, axis=0, keepdims=True))
        alpha = jnp.exp2(m_old - m_new)
        m_ref[...] = m_new

        pv = p_times_v(jnp.minimum(j + 1, nq - 1), p_src)
        acc_ref[...] = (acc_ref[...] + pv) * alpha

        for c in range(t // rb):
            p_dst[c * rb:(c + 1) * rb, :] = jnp.exp2((rows(c) - m_new).astype(BF16))
        if dst is src:
            scores(j_next, dst)

    buf_a = (sa_ref, mxa_ref)
    buf_b = (sb_ref, mxb_ref)

    def single(n, bias_idx):
        step(n, bias_idx, buf_a, buf_a, pa_ref, pa_ref)

    def pair(n, bias_idx0, bias_idx1):
        step(n, bias_idx0, buf_a, buf_b, pa_ref, pb_ref)
        step(n + 1, bias_idx1, buf_b, buf_a, pb_ref, pa_ref)

    scores(i, buf_a)

    @pl.when(i == 0)
    def _():
        single(0, 0)

    @pl.when(i >= 1)
    def _():
        pair(0, 0, 1)

    n_far = jnp.maximum(i - 1, 0)

    @pl.when(n_far % 2 == 1)
    def _():
        single(2, None)

    def far_pair(c, carry):
        pair(2 + n_far % 2 + 2 * c, None, None)
        return carry

    lax.fori_loop(0, n_far // 2, far_pair, 0)

    acc = acc_ref[...] + p_times_v(0, pa_ref)
    o = acc[0:DIFF_VD, :] / acc[DIFF_VD:DIFF_VD + 1, :]
    lam = _lam_value(lam_ref, lam_init)
    a = o[:, :t] - lam * o[:, t:]
    a = a * lax.rsqrt(jnp.mean(a * a, axis=0, keepdims=True) + LN_EPS)
    o_ref[...] = (a.T * subg_ref[...] * (1.0 - lam_init)).astype(o_ref.dtype)


def _attn_prompt(q, kb, vt, rel_bias, diff_lam, sub_g, layer, lam_init, bsz, seq, t):
    tv = vt.shape[-1]
    assert t >= MAX_DISTANCE and t % CHUNK == 0 and seq % t == 0 and t % tv == 0
    n = q.shape[0]
    nq = seq // t
    far_bias = rel_bias[_rel_bucket(jnp.full((), -MAX_DISTANCE, jnp.int32))].astype(F32)[:, None, None]
    b0 = _toeplitz_bias(rel_bias, lambda d: -d, t, t) - far_bias
    b1 = _toeplitz_bias(rel_bias, lambda d: -d - t, t, t) - far_bias
    r = jnp.arange(t, dtype=jnp.int32)[:, None]
    c = jnp.arange(t, dtype=jnp.int32)[None, :]
    b0 = jnp.where((r // CHUNK) <= (c // CHUNK), b0 * LOG2E, MASKED)
    tiles = jnp.stack([b0, b1 * LOG2E], axis=1)
    tiles = jnp.concatenate([tiles, tiles], axis=-1)

    return pl.pallas_call(
        functools.partial(_attn_prompt_body, t, tv, nq, lam_init),
        grid=(bsz, N_DIFF, nq),
        in_specs=[
            pl.BlockSpec((t, DIFF_VD), lambda b, h, i: (b * nq + i, h)),
            pl.BlockSpec((seq, DIFF_VD), lambda b, h, i: (b, h)),
            pl.BlockSpec((None, None, seq // tv, VT_ROWS, tv), lambda b, h, i: (b, h, 0, 0, 0)),
            pl.BlockSpec((None, 2, t, 2 * t), lambda b, h, i: (h, 0, 0, 0)),
            pl.BlockSpec((None, 4, DIFF_HD), lambda b, h, i: (layer, 0, 0)),
            pl.BlockSpec((None, 1, DIFF_VD), lambda b, h, i: (layer, 0, 0)),
        ],
        out_specs=pl.BlockSpec((t, DIFF_VD), lambda b, h, i: (b * nq + i, h)),
        out_shape=jax.ShapeDtypeStruct((n, D_ATT), BF16),
        scratch_shapes=[
            pltpu.VMEM((2 * t, DIFF_VD), BF16),
            pltpu.VMEM((t, 2 * t), F32),
            pltpu.VMEM((t, 2 * t), F32),
            pltpu.VMEM((8, 2 * t), F32),
            pltpu.VMEM((8, 2 * t), F32),
            pltpu.VMEM((t, 2 * t), BF16),
            pltpu.VMEM((t, 2 * t), BF16),
            pltpu.VMEM((1, 2 * t), F32),
            pltpu.VMEM((VT_ROWS, 2 * t), F32),
        ],
        compiler_params=_cparams(("parallel", "parallel", "arbitrary"), V7X_VMEM_LIMIT),
        name="attn_prompt",
    )(q, kb, vt, tiles, diff_lam, sub_g.reshape(-1, 1, DIFF_VD))


def _attn_sample_body(tq, lam_init, q_ref, kp_ref, kn_ref, vp_ref, vn_ref, bp_ref, bn_ref, lam_ref, subg_ref, o_ref):
    qbd = _block_diag_q(q_ref[...])
    s_p = _dot_nt(qbd, kp_ref[...].astype(BF16)) + bp_ref[...]
    s_n = _dot_nt(qbd, kn_ref[...]) + bn_ref[...]
    m = jnp.maximum(jnp.max(s_p, axis=1, keepdims=True), jnp.max(s_n, axis=1, keepdims=True))
    p_p = jnp.exp2(s_p - m)
    p_n = jnp.exp2(s_n - m)
    l = jnp.sum(p_p, axis=1, keepdims=True) + jnp.sum(p_n, axis=1, keepdims=True)
    o = (_dot(p_p.astype(BF16), vp_ref[...].astype(BF16)) + _dot(p_n.astype(BF16), vn_ref[...].astype(BF16))) / l
    lam = _lam_value(lam_ref, lam_init)
    a = o[:tq, :] - lam * o[tq:, :]
    a = a * lax.rsqrt(jnp.mean(a * a, axis=1, keepdims=True) + LN_EPS)
    o_ref[...] = (a * subg_ref[...] * (1.0 - lam_init)).astype(o_ref.dtype)


def _attn_sample(q, kb, vf, cache_k, cache_v, rel_bias, diff_lam, sub_g, layer, lam_init, bsz, tq):
    assert tq <= 128
    n = q.shape[0]
    past = cache_k.shape[2]
    bias = _toeplitz_bias(rel_bias, lambda d: d - past, tq, past + tq) * LOG2E
    qpos = past + jnp.arange(tq, dtype=jnp.int32)[:, None]
    kpos = jnp.arange(past + tq, dtype=jnp.int32)[None, :]
    bias = jnp.where((kpos // CHUNK) <= (qpos // CHUNK), bias, MASKED)
    bias = jnp.concatenate([bias, bias], axis=1)
    cache_k = cache_k.reshape(cache_k.shape[:3] + (D_ATT,))
    cache_v = cache_v.reshape(cache_v.shape[:3] + (D_ATT,))
    cache_spec = pl.BlockSpec((None, None, past, DIFF_VD), lambda b, h: (layer, b, 0, h))
    new_spec = pl.BlockSpec((tq, DIFF_VD), lambda b, h: (b, h))
    return pl.pallas_call(
        functools.partial(_attn_sample_body, tq, lam_init),
        grid=(bsz, N_DIFF),
        in_specs=[
            new_spec, cache_spec, new_spec, cache_spec, new_spec,
            pl.BlockSpec((None, 2 * tq, past), lambda b, h: (h, 0, 0)),
            pl.BlockSpec((None, 2 * tq, tq), lambda b, h: (h, 0, 0)),
            pl.BlockSpec((None, 4, DIFF_HD), lambda b, h: (layer, 0, 0)),
            pl.BlockSpec((None, 1, DIFF_VD), lambda b, h: (layer, 0, 0)),
        ],
        out_specs=new_spec,
        out_shape=jax.ShapeDtypeStruct((n, D_ATT), BF16),
        compiler_params=_cparams(("parallel", "parallel")),
        name="attn_sample",
    )(q, cache_k, kb, cache_v, vf, bias[:, :, :past], bias[:, :, past:], diff_lam, sub_g.reshape(-1, 1, DIFF_VD))


def _retention_body(nc, q_ref, k_ref, kw_ref, v_ref, gate_ref, s0_ref, decay_ref, wq_ref, gl_ref,
                    o_ref, snew_ref, s_ref):
    c = pl.program_id(2)

    @pl.when(c == 0)
    def _():
        s_ref[...] = s0_ref[...]

    q = q_ref[...]
    v = v_ref[...]
    s_prev = s_ref[...]
    inner = _dot_nt(q, k_ref[...]) * decay_ref[...]
    o = _dot(inner.astype(BF16), v) + _dot(q, s_prev.astype(BF16)) * wq_ref[...]
    s_new = gl_ref[...] * s_prev + _dot_tn(kw_ref[...], v)
    s_ref[...] = s_new

    @pl.when(c == nc - 1)
    def _():
        snew_ref[...] = s_new

    mu = jnp.mean(o, axis=1, keepdims=True)
    d = o - mu
    var = jnp.mean(d * d, axis=1, keepdims=True)
    o_ref[...] = (gate_ref[...] * (d * lax.rsqrt(var + LN_EPS))).astype(o_ref.dtype)


def _retention_tables(lc):
    log_g = jnp.log1p(-jnp.exp2(-5.0 - jnp.arange(N_RET, dtype=F32)))
    idx = jnp.arange(lc, dtype=F32)
    rel = idx[:, None] - idx[None, :]
    decay = jnp.where(rel >= 0, jnp.exp(jnp.maximum(rel, 0.0)[None] * log_g[:, None, None]), 0.0)
    w_k = jnp.exp((lc - 1.0 - idx)[None, :] * log_g[:, None])
    w_q = jnp.exp((idx + 1.0)[None, :] * log_g[:, None])
    g_l = jnp.exp(lc * log_g)
    widen = lambda w: jnp.repeat(w.T, RET_HD, axis=1)
    gl_tab = jnp.broadcast_to(g_l[:, None, None], (N_RET, 1, RET_HD))
    return decay, widen(w_q), widen(w_k), gl_tab


def _retention(rq, rk, rkw, rv, gate, s0, decay, wq_tab, gl_tab, bsz, t, lc):
    n = rq.shape[0]
    nc = t // lc
    tile = pl.BlockSpec((lc, RET_HD), lambda b, h, c: (b * nc + c, h))
    state = pl.BlockSpec((None, None, RET_HD, RET_HD), lambda b, h, c: (b, h, 0, 0))
    return pl.pallas_call(
        functools.partial(_retention_body, nc),
        grid=(bsz, N_RET, nc),
        in_specs=[
            tile, tile, tile, tile, tile, state,
            pl.BlockSpec((None, lc, lc), lambda b, h, c: (h, 0, 0)),
            pl.BlockSpec((lc, RET_HD), lambda b, h, c: (0, h)),
            pl.BlockSpec((None, 1, RET_HD), lambda b, h, c: (h, 0, 0)),
        ],
        out_specs=[tile, state],
        out_shape=[
            jax.ShapeDtypeStruct((n, D_RET), BF16),
            jax.ShapeDtypeStruct((bsz, N_RET, RET_HD, RET_HD), F32),
        ],
        scratch_shapes=[pltpu.VMEM((RET_HD, RET_HD), F32)],
        compiler_params=_cparams(("parallel", "parallel", "arbitrary")),
        name="retention",
    )(rq, rk, rkw, rv, gate, s0, decay, wq_tab, gl_tab)


def _out_proj_ln_body(alpha, x_ref, c_ref, a_ref, r_ref, w_ref, g_ref, b_ref, o_ref):
    e0 = D_CONV
    e1 = D_CONV + D_ATT
    mix = (_dot(c_ref[...], w_ref[0:e0, :]) + _dot(a_ref[...], w_ref[e0:e1, :])
           + _dot(r_ref[...], w_ref[e1:e1 + D_RET, :]))
    o_ref[...] = _layer_norm_rows(alpha * x_ref[...] + mix, g_ref[...], b_ref[...])


def _out_proj_ln(x, conv_out, attn_out, ret_out, w_out, g, b, layer, alpha, tm):
    n, d = x.shape
    d_mix = D_CONV + D_ATT + D_RET
    tile = lambda w: pl.BlockSpec((tm, w), lambda i: (i, 0))
    return pl.pallas_call(
        functools.partial(_out_proj_ln_body, alpha),
        grid=(n // tm,),
        in_specs=[
            tile(d), tile(D_CONV), tile(D_ATT), tile(D_RET),
            pl.BlockSpec((None, d_mix, d), lambda i: (layer, 0, 0), pipeline_mode=pl.Buffered(1)),
            pl.BlockSpec((1, d), lambda i: (0, 0)),
            pl.BlockSpec((1, d), lambda i: (0, 0)),
        ],
        out_specs=tile(d),
        out_shape=jax.ShapeDtypeStruct((n, d), F32),
        compiler_params=_cparams(("parallel",), V7X_VMEM_LIMIT),
        name="out_proj_ln",
    )(x, conv_out, attn_out, ret_out, w_out, g, b)


def _rope_tables(pos0, t):
    half = RET_HD // 2
    inv = 1.0 / (ROPE_BASE ** (jnp.arange(half, dtype=F32) / half))
    ang = (pos0 + jnp.arange(t, dtype=jnp.int32)).astype(F32)[:, None] * inv[None, :]
    cos = jnp.cos(ang)
    sin = jnp.sin(ang)
    return jnp.concatenate([cos, cos], axis=1), jnp.concatenate([-sin, sin], axis=1)


def _trunk(x, pos0, caches, params, tiles):
    (ln_g, ln_b, wg, wu, wd, w_in, w_out, conv_w, conv_b, conv_ln_g, conv_ln_b, diff_lam, diff_sub_g,
     rel_bias) = params
    bsz, t, d = x.shape
    depth = ln_g.shape[0]
    alpha = (2 * depth) ** 0.25
    tm_ffn, tf, tm_tok, tm_out, t_att = tiles
    prompt = caches is None
    lc = tm_tok
    cos, sin = _rope_tables(pos0, t)
    decay, wq_tab, wk_tab, gl_tab = _retention_tables(lc)
    x = x.reshape(bsz * t, d)
    conv_l, k_l, v_l, s_l = [], [], [], []
    for l in range(depth):
        lam_init = 0.8 - 0.6 * math.exp(-0.3 * l)
        if prompt:
            conv_prev = jnp.zeros((bsz, CONV_WIDTH - 1, D_CONV), F32)
            ret_prev = jnp.zeros((bsz, N_RET, RET_HD, RET_HD), F32)
        else:
            cache_conv, cache_k, cache_v, state_ret = caches
            conv_prev, ret_prev = cache_conv[l], state_ret[l]
        x = _ffn_ln(x, wg, wu, wd, ln_g[l, 0:1], ln_b[l, 0:1], l, 0, alpha, tm_ffn, tf)
        u, q, kf, kb, vf, vt, rq, rk, rkw, rv, gate = _in_proj(x, w_in, l, cos, sin, wk_tab, bsz, t, tm_tok)
        conv_out, conv_new = _conv_module(u, conv_prev, conv_w, conv_b, conv_ln_g, conv_ln_b, l, bsz, t, tm_tok)
        if prompt:
            attn_out = _attn_prompt(q, kb, vt, rel_bias, diff_lam, diff_sub_g, l, lam_init, bsz, t, t_att)
        else:
            attn_out = _attn_sample(q, kb, vf, cache_k, cache_v, rel_bias, diff_lam, diff_sub_g, l, lam_init, bsz, t)
        ret_out, s_new = _retention(rq, rk, rkw, rv, gate, ret_prev, decay, wq_tab, gl_tab, bsz, t, lc)
        x = _out_proj_ln(x, conv_out, attn_out, ret_out, w_out, ln_g[l, 1:2], ln_b[l, 1:2], l, alpha, tm_out)
        x = _ffn_ln(x, wg, wu, wd, ln_g[l, 2:3], ln_b[l, 2:3], l, 1, alpha, tm_ffn, tf)
        conv_l.append(conv_new)
        k_l.append(kf.reshape(bsz, t, N_DIFF, DIFF_VD))
        v_l.append(vf.reshape(bsz, t, N_DIFF, DIFF_VD))
        s_l.append(s_new)
    return x.reshape(bsz, t, d), jnp.stack(conv_l), jnp.stack(k_l), jnp.stack(v_l), jnp.stack(s_l)


def _pick(n, cap):
    if n <= cap:
        return n
    best = None
    for c in range(128, cap + 1, 128):
        if n % c == 0:
            best = c
    assert best is not None
    return best


def kernel(x_prompt, x_sample, cache_conv, cache_k, cache_v, state_ret, ln_g, ln_b, ffn_w_gate, ffn_w_up,
           ffn_w_down, w_in, w_out, conv_w, conv_b, conv_ln_g, conv_ln_b, diff_lam, diff_sub_g, rel_bias):
    params = (ln_g, ln_b, ffn_w_gate.astype(BF16), ffn_w_up.astype(BF16), ffn_w_down.astype(BF16),
              w_in.astype(BF16), w_out.astype(BF16), conv_w, conv_b, conv_ln_g, conv_ln_b, diff_lam, diff_sub_g,
              rel_bias)
    tf = _pick(ffn_w_gate.shape[-1], 512)
    bp, tp, _ = x_prompt.shape
    bs, ts, _ = x_sample.shape
    tok_p = _pick(tp, 256)
    y_p, conv_p, k_p, v_p, ret_p = _trunk(
        x_prompt, 0, None, params, (_pick(bp * tp, 512), tf, tok_p, _pick(bp * tp, 512), _pick(tp, 512)))
    y_s, conv_s, k_s, v_s, ret_s = _trunk(
        x_sample, cache_k.shape[2], (cache_conv, cache_k, cache_v, state_ret), params,
        (_pick(bs * ts, 512), tf, ts, _pick(bs * ts, 512), None))
    return (y_p, y_s, conv_p, k_p, v_p, ret_p, conv_s, k_s, v_s, ret_s)
```

```python
import functools
import math

import jax
import jax.numpy as jnp
import numpy as np
from jax import lax
from jax.experimental import pallas as pl
from jax.experimental.pallas import tpu as pltpu

CHUNK = 64
CONV_WIDTH = 31
D_CONV = 512
N_DIFF = 8
DIFF_HD = 64
DIFF_VD = 2 * DIFF_HD
N_RET = 4
RET_HD = 128
N_BUCKETS = 32
MAX_DISTANCE = 128
LN_EPS = 1e-5
ROPE_BASE = 10000.0

D_ATT = N_DIFF * DIFF_VD
VT_ROWS = DIFF_VD + 16
D_RET = N_RET * RET_HD
C_CA, C_CG = 0, D_CONV
C_Q = 2 * D_CONV
C_K = C_Q + D_ATT
C_V = C_K + D_ATT
C_RQ = C_V + D_ATT
C_RK = C_RQ + D_RET
C_RV = C_RK + D_RET
C_RG = C_RV + D_RET
D_IN = C_RG + D_RET

LOG2E = math.log2(math.e)
MASKED = -1e30
ATTN_ROW_BLOCK = 32
CONV_PAD = 32
V7X_VMEM_LIMIT = 56 * 1024 * 1024

F32 = jnp.float32
BF16 = jnp.bfloat16


def _cparams(sem, vmem=None):
    return pltpu.CompilerParams(dimension_semantics=sem, vmem_limit_bytes=vmem)


def _sigmoid(x):
    return 1.0 / (1.0 + jnp.exp(-x))


def _layer_norm_rows(y, g, b):
    mu = jnp.mean(y, axis=-1, keepdims=True)
    d = y - mu
    var = jnp.mean(d * d, axis=-1, keepdims=True)
    return d * lax.rsqrt(var + LN_EPS) * g + b


def _dot(a, b):
    return jnp.dot(a, b, preferred_element_type=F32)


def _dot_nt(a, b):
    return lax.dot_general(a, b, (((1,), (1,)), ((), ())), preferred_element_type=F32)


def _dot_tn(a, b):
    return lax.dot_general(a, b, (((0,), (0,)), ((), ())), preferred_element_type=F32)


def _ffn_ln_body(alpha, nf, x_ref, wg_ref, wu_ref, wd_ref, g_ref, b_ref, o_ref, xb_ref):
    f = pl.program_id(1)

    @pl.when(f == 0)
    def _():
        xb_ref[...] = x_ref[...].astype(BF16)
        o_ref[...] = jnp.zeros_like(o_ref)

    xb = xb_ref[...]
    hg = _dot(xb, wg_ref[...])
    hu = _dot(xb, wu_ref[...])
    act = (hg * _sigmoid(hg) * hu).astype(BF16)
    o_ref[...] += _dot(act, wd_ref[...])

    @pl.when(f == nf - 1)
    def _():
        y = alpha * x_ref[...] + 0.5 * o_ref[...]
        o_ref[...] = _layer_norm_rows(y, g_ref[...], b_ref[...])


def _ffn_ln(x, wg, wu, wd, g, b, layer, slot, alpha, tm):
    n, d = x.shape
    nf, tf = wg.shape[2], wg.shape[4]
    return pl.pallas_call(
        functools.partial(_ffn_ln_body, alpha, nf),
        grid=(n // tm, nf),
        in_specs=[
            pl.BlockSpec((tm, d), lambda i, f: (i, 0)),
            pl.BlockSpec((None, None, None, d, tf), lambda i, f: (layer, slot, f, 0, 0)),
            pl.BlockSpec((None, None, None, d, tf), lambda i, f: (layer, slot, f, 0, 0)),
            pl.BlockSpec((None, None, tf, d), lambda i, f: (layer, slot, f, 0)),
            pl.BlockSpec((1, d), lambda i, f: (0, 0)),
            pl.BlockSpec((1, d), lambda i, f: (0, 0)),
        ],
        out_specs=pl.BlockSpec((tm, d), lambda i, f: (i, 0)),
        out_shape=jax.ShapeDtypeStruct((n, d), F32),
        scratch_shapes=[pltpu.VMEM((tm, d), BF16)],
        compiler_params=_cparams(("parallel", "arbitrary"), V7X_VMEM_LIMIT),
        name="ffn_ln",
    )(x, wg, wu, wd, g, b)


def _rope_heads(x, cos, sin):
    outs = []
    for h in range(N_RET):
        xh = x[:, h * RET_HD:(h + 1) * RET_HD]
        outs.append(xh * cos + pltpu.roll(xh, RET_HD // 2, 1) * sin)
    return jnp.concatenate(outs, axis=1)


def _in_proj_body(tm, h_ref, w_ref, cos_ref, sin_ref, wk_ref,
                  u_ref, q_ref, kf_ref, kb_ref, vf_ref, vt_ref, rq_ref, rk_ref, rkw_ref, rv_ref, rg_ref):
    hb = h_ref[...].astype(BF16)

    def mm(lo, hi):
        return _dot(hb, w_ref[:, lo:hi])

    u_ref[...] = mm(C_CA, C_CG) * _sigmoid(mm(C_CG, C_Q))
    q_ref[...] = (mm(C_Q, C_K) * (DIFF_HD ** -0.5 * LOG2E)).astype(BF16)
    k = mm(C_K, C_V)
    kf_ref[...] = k
    kb_ref[...] = k.astype(BF16)
    v = mm(C_V, C_RQ)
    vf_ref[...] = v
    vt_ref[:, 0:DIFF_VD, :] = v.T.reshape(N_DIFF, DIFF_VD, tm).astype(BF16)
    pad_row = lax.broadcasted_iota(jnp.int32, (N_DIFF, VT_ROWS - DIFF_VD, tm), 1)
    vt_ref[:, DIFF_VD:VT_ROWS, :] = jnp.where(pad_row == 0, 1.0, 0.0).astype(BF16)
    cos = cos_ref[...]
    sin = sin_ref[...]
    rq_ref[...] = _rope_heads(mm(C_RQ, C_RK), cos, sin).astype(BF16)
    rk = _rope_heads(mm(C_RK, C_RV), cos, sin) * (RET_HD ** -0.5)
    rk_ref[...] = rk.astype(BF16)
    rkw_ref[...] = (rk * wk_ref[...]).astype(BF16)
    rv_ref[...] = mm(C_RV, C_RG).astype(BF16)
    rg = mm(C_RG, D_IN)
    rg_ref[...] = rg * _sigmoid(rg)


def _in_proj(h, w_in, layer, cos, sin, wk_tab, bsz, t, tm):
    n, d = h.shape
    nt = t // tm
    row = lambda b, i: (b * nt + i, 0)
    outs = [
        ((n, D_CONV), F32),
        ((n, D_ATT), BF16),
        ((n, D_ATT), F32),
        ((n, D_ATT), BF16),
        ((n, D_ATT), F32),
        None,
        ((n, D_RET), BF16),
        ((n, D_RET), BF16),
        ((n, D_RET), BF16),
        ((n, D_RET), BF16),
        ((n, D_RET), F32),
    ]
    out_shape, out_specs = [], []
    for o in outs:
        if o is None:
            out_shape.append(jax.ShapeDtypeStruct((bsz, N_DIFF, nt, VT_ROWS, tm), BF16))
            out_specs.append(pl.BlockSpec((None, N_DIFF, None, VT_ROWS, tm), lambda b, i: (b, 0, i, 0, 0)))
        else:
            out_shape.append(jax.ShapeDtypeStruct(*o))
            out_specs.append(pl.BlockSpec((tm, o[0][1]), row))
    return pl.pallas_call(
        functools.partial(_in_proj_body, tm),
        grid=(bsz, nt),
        in_specs=[
            pl.BlockSpec((tm, d), row),
            pl.BlockSpec((None, d, D_IN), lambda b, i: (layer, 0, 0), pipeline_mode=pl.Buffered(1)),
            pl.BlockSpec((tm, RET_HD), lambda b, i: (i, 0)),
            pl.BlockSpec((tm, RET_HD), lambda b, i: (i, 0)),
            pl.BlockSpec((tm, D_RET), lambda b, i: (0, 0)),
        ],
        out_specs=out_specs,
        out_shape=out_shape,
        compiler_params=_cparams(("parallel", "parallel"), V7X_VMEM_LIMIT),
        name="in_proj",
    )(h, w_in, cos, sin, wk_tab)


def _conv_body(tm, nt, u_ref, prev_ref, w_ref, cb_ref, g_ref, b_ref, o_ref, new_ref, ext_ref):
    i = pl.program_id(1)
    hist = CONV_WIDTH - 1

    @pl.when(i == 0)
    def _():
        ext_ref[0:CONV_PAD - hist, :] = jnp.zeros((CONV_PAD - hist, D_CONV), F32)
        ext_ref[CONV_PAD - hist:CONV_PAD, :] = prev_ref[...]

    @pl.when(i > 0)
    def _():
        ext_ref[0:CONV_PAD, :] = ext_ref[tm:tm + CONV_PAD, :]

    ext_ref[CONV_PAD:CONV_PAD + tm, :] = u_ref[...]

    rb = 32
    cb = cb_ref[...]
    g = g_ref[...]
    b = b_ref[...]
    for r0 in range(0, tm, rb):
        acc = jnp.zeros((rb, D_CONV), F32)
        for w in range(CONV_WIDTH):
            start = CONV_PAD - hist + w + r0
            acc = acc + ext_ref[start:start + rb, :] * w_ref[w:w + 1, :]
        y = _layer_norm_rows(acc + cb, g, b)
        o_ref[r0:r0 + rb, :] = (y * _sigmoid(y)).astype(o_ref.dtype)

    @pl.when(i == nt - 1)
    def _():
        new_ref[...] = ext_ref[CONV_PAD + tm - hist:CONV_PAD + tm, :]


def _conv_module(u, prev, conv_w, conv_b, g, b, layer, bsz, t, tm):
    n = u.shape[0]
    nt = t // tm
    assert t >= CONV_PAD and tm >= CONV_PAD and tm % 32 == 0
    vec = lambda: pl.BlockSpec((None, 1, D_CONV), lambda bb, i: (layer, 0, 0))
    return pl.pallas_call(
        functools.partial(_conv_body, tm, nt),
        grid=(bsz, nt),
        in_specs=[
            pl.BlockSpec((tm, D_CONV), lambda bb, i: (bb * nt + i, 0)),
            pl.BlockSpec((None, CONV_WIDTH - 1, D_CONV), lambda bb, i: (bb, 0, 0)),
            pl.BlockSpec((None, CONV_WIDTH, D_CONV), lambda bb, i: (layer, 0, 0)),
            vec(), vec(), vec(),
        ],
        out_specs=[
            pl.BlockSpec((tm, D_CONV), lambda bb, i: (bb * nt + i, 0)),
            pl.BlockSpec((None, CONV_WIDTH - 1, D_CONV), lambda bb, i: (bb, 0, 0)),
        ],
        out_shape=[
            jax.ShapeDtypeStruct((n, D_CONV), BF16),
            jax.ShapeDtypeStruct((bsz, CONV_WIDTH - 1, D_CONV), F32),
        ],
        scratch_shapes=[pltpu.VMEM((tm + CONV_PAD, D_CONV), F32)],
        compiler_params=_cparams(("parallel", "arbitrary")),
        name="conv_module",
    )(u, prev, conv_w, conv_b.reshape(-1, 1, D_CONV), g.reshape(-1, 1, D_CONV), b.reshape(-1, 1, D_CONV))


def _rel_bucket(rel):
    half = N_BUCKETS // 2
    max_exact = half // 2
    ret = jnp.where(rel > 0, half, 0)
    n = jnp.abs(rel)
    nf = jnp.maximum(n, 1).astype(F32)
    large = max_exact + (jnp.log(nf / max_exact) / math.log(MAX_DISTANCE / max_exact)
                         * (half - max_exact)).astype(jnp.int32)
    large = jnp.minimum(large, half - 1)
    return ret + jnp.where(n < max_exact, n, large)


def _toeplitz_bias(rel_bias, rel_of_diff, rows, cols):
    period = rows + cols
    x = np.arange(period)
    diff = np.where(x < cols, x, x - period)
    rel = jnp.asarray(rel_of_diff(diff), jnp.int32)
    wp = rel_bias[_rel_bucket(rel)].astype(F32).T
    flat = jnp.tile(wp, (1, rows))[:, :rows * (period - 1)]
    return flat.reshape(-1, rows, period - 1)[:, :, :cols]


def _lam_value(lam_ref, lam_init):
    lf = lam_ref[...]
    a = jnp.sum(lf[0:1, :] * lf[1:2, :], axis=1, keepdims=True)
    b = jnp.sum(lf[2:3, :] * lf[3:4, :], axis=1, keepdims=True)
    return jnp.exp(a) - jnp.exp(b) + lam_init


def _block_diag_q(q):
    lane = lax.broadcasted_iota(jnp.int32, q.shape, 1)
    zero = jnp.zeros_like(q)
    return jnp.concatenate([jnp.where(lane < DIFF_HD, q, zero), jnp.where(lane >= DIFF_HD, q, zero)], axis=0)


def _attn_prompt_body(t, tv, nq, lam_init, q_ref, k_ref, vt_ref, bias_ref, lam_ref, subg_ref, o_ref,
                      qbd_ref, sa_ref, sb_ref, pa_ref, pb_ref, m_ref, mprev_ref, acc_ref):
    i = pl.program_id(2)
    sub = t // tv
    rb = ATTN_ROW_BLOCK
    lanes = 2 * t
    q_t = q_ref[...].astype(F32).T.astype(BF16)
    head_row = lax.broadcasted_iota(jnp.int32, q_t.shape, 0)
    zero = jnp.zeros_like(q_t)
    qbd_ref[:, 0:t] = jnp.where(head_row < DIFF_HD, q_t, zero)
    qbd_ref[:, t:lanes] = jnp.where(head_row >= DIFF_HD, q_t, zero)
    acc_ref[...] = jnp.zeros_like(acc_ref)
    pa_ref[...] = jnp.zeros_like(pa_ref)

    def biased_rows(s_ref, bias_idx, c):
        sc = s_ref[c * rb:(c + 1) * rb, :]
        if bias_idx is not None:
            bias = bias_ref[bias_idx, c * rb:(c + 1) * rb, :]
            sc = sc + jnp.concatenate([bias, bias], axis=1)
        return sc

    def raw_scores(j):
        k = k_ref[pl.ds(pl.multiple_of(j * t, t), t), :]
        return _dot(k, qbd_ref[...])

    def scores(j, s_dst, bias_idx, m_old, s=None):
        if s is None:
            s = raw_scores(j)
        s_dst[...] = s
        if bias_idx is None:
            mx = jnp.max(s.reshape(t // 8, 8, lanes), axis=0)
        else:
            mx = None
            for c in range(t // rb):
                cm = jnp.max(biased_rows(s_dst, bias_idx, c).reshape(rb // 8, 8, lanes), axis=0)
                mx = cm if mx is None else jnp.maximum(mx, cm)
        return jnp.maximum(m_old, jnp.max(mx, axis=0, keepdims=True))

    def p_times_v(j, p_ref):
        out = None
        for u in range(sub):
            part = _dot(vt_ref[j * sub + u], p_ref[u * tv:(u + 1) * tv, :])
            out = part if out is None else out + part
        return out

    def step(n, bias_idx, next_bias_idx, s_src, s_dst, p_src, p_dst):
        j = i - n
        j_next = jnp.maximum(j - 1, 0)
        m_cur = m_ref[...]
        if s_dst is not s_src:
            s_next = raw_scores(j_next)
        alpha = jnp.exp2(mprev_ref[...] - m_cur)
        pv = p_times_v(jnp.minimum(j + 1, nq - 1), p_src)
        acc_ref[...] = (acc_ref[...] + pv) * alpha
        for c in range(t // rb):
            p_dst[c * rb:(c + 1) * rb, :] = jnp.exp2((biased_rows(s_src, bias_idx, c) - m_cur).astype(BF16))
        if s_dst is s_src:
            m_next = scores(j_next, s_dst, next_bias_idx, m_cur)
        else:
            m_next = scores(j_next, s_dst, next_bias_idx, m_cur, s_next)
        mprev_ref[...] = m_cur
        m_ref[...] = m_next

    def single(n, bias_idx, next_bias_idx):
        step(n, bias_idx, next_bias_idx, sa_ref, sa_ref, pa_ref, pa_ref)

    def pair(n, bias_idx0, bias_idx1):
        step(n, bias_idx0, bias_idx1, sa_ref, sb_ref, pa_ref, pb_ref)
        step(n + 1, bias_idx1, None, sb_ref, sa_ref, pb_ref, pa_ref)

    masked = jnp.full(m_ref.shape, MASKED, F32)
    mprev_ref[...] = masked
    m_ref[...] = scores(i, sa_ref, 0, masked)

    @pl.when(i == 0)
    def _():
        single(0, 0, None)

    @pl.when(i >= 1)
    def _():
        pair(0, 0, 1)

    n_far = jnp.maximum(i - 1, 0)

    @pl.when(n_far % 2 == 1)
    def _():
        single(2, None, None)

    def far_pair(c, carry):
        pair(2 + n_far % 2 + 2 * c, None, None)
        return carry

    lax.fori_loop(0, n_far // 2, far_pair, 0)

    acc = acc_ref[...] + p_times_v(0, pa_ref)
    o = acc[0:DIFF_VD, :] / acc[DIFF_VD:DIFF_VD + 1, :]
    lam = _lam_value(lam_ref, lam_init)
    a = o[:, :t] - lam * o[:, t:]
    a = a * lax.rsqrt(jnp.mean(a * a, axis=0, keepdims=True) + LN_EPS)
    o_ref[...] = (a.T * subg_ref[...] * (1.0 - lam_init)).astype(o_ref.dtype)


def _attn_prompt(q, kb, vt, rel_bias, diff_lam, sub_g, layer, lam_init, bsz, seq, t):
    tv = vt.shape[-1]
    assert t >= MAX_DISTANCE and t % CHUNK == 0 and seq % t == 0 and t % tv == 0
    n = q.shape[0]
    nq = seq // t
    far_bias = rel_bias[_rel_bucket(jnp.full((), -MAX_DISTANCE, jnp.int32))].astype(F32)[:, None, None]
    b0 = _toeplitz_bias(rel_bias, lambda d: -d, t, t) - far_bias
    b1 = _toeplitz_bias(rel_bias, lambda d: -d - t, t, t) - far_bias
    r = jnp.arange(t, dtype=jnp.int32)[:, None]
    c = jnp.arange(t, dtype=jnp.int32)[None, :]
    b0 = jnp.where((r // CHUNK) <= (c // CHUNK), b0 * LOG2E, MASKED)
    tiles = jnp.stack([b0, b1 * LOG2E], axis=1)

    return pl.pallas_call(
        functools.partial(_attn_prompt_body, t, tv, nq, lam_init),
        grid=(bsz, N_DIFF, nq),
        in_specs=[
            pl.BlockSpec((t, DIFF_VD), lambda b, h, i: (b * nq + i, h)),
            pl.BlockSpec((seq, DIFF_VD), lambda b, h, i: (b, h)),
            pl.BlockSpec((None, None, seq // tv, VT_ROWS, tv), lambda b, h, i: (b, h, 0, 0, 0)),
            pl.BlockSpec((None, 2, t, t), lambda b, h, i: (h, 0, 0, 0), pipeline_mode=pl.Buffered(1)),
            pl.BlockSpec((None, 4, DIFF_HD), lambda b, h, i: (layer, 0, 0)),
            pl.BlockSpec((None, 1, DIFF_VD), lambda b, h, i: (layer, 0, 0)),
        ],
        out_specs=pl.BlockSpec((t, DIFF_VD), lambda b, h, i: (b * nq + i, h)),
        out_shape=jax.ShapeDtypeStruct((n, D_ATT), BF16),
        scratch_shapes=[
            pltpu.VMEM((DIFF_VD, 2 * t), BF16),
            pltpu.VMEM((t, 2 * t), F32),
            pltpu.VMEM((t, 2 * t), F32),
            pltpu.VMEM((t, 2 * t), BF16),
            pltpu.VMEM((t, 2 * t), BF16),
            pltpu.VMEM((1, 2 * t), F32),
            pltpu.VMEM((1, 2 * t), F32),
            pltpu.VMEM((VT_ROWS, 2 * t), F32),
        ],
        compiler_params=_cparams(("parallel", "parallel", "arbitrary"), V7X_VMEM_LIMIT),
        name="attn_prompt",
    )(q, kb, vt, tiles, diff_lam, sub_g.reshape(-1, 1, DIFF_VD))


def _attn_sample_body(tq, lam_init, q_ref, kp_ref, kn_ref, vp_ref, vn_ref, bp_ref, bn_ref, lam_ref, subg_ref, o_ref):
    qbd = _block_diag_q(q_ref[...])
    s_p = _dot_nt(qbd, kp_ref[...].astype(BF16)) + bp_ref[...]
    s_n = _dot_nt(qbd, kn_ref[...]) + bn_ref[...]
    m = jnp.maximum(jnp.max(s_p, axis=1, keepdims=True), jnp.max(s_n, axis=1, keepdims=True))
    p_p = jnp.exp2(s_p - m)
    p_n = jnp.exp2(s_n - m)
    l = jnp.sum(p_p, axis=1, keepdims=True) + jnp.sum(p_n, axis=1, keepdims=True)
    o = (_dot(p_p.astype(BF16), vp_ref[...].astype(BF16)) + _dot(p_n.astype(BF16), vn_ref[...].astype(BF16))) / l
    lam = _lam_value(lam_ref, lam_init)
    a = o[:tq, :] - lam * o[tq:, :]
    a = a * lax.rsqrt(jnp.mean(a * a, axis=1, keepdims=True) + LN_EPS)
    o_ref[...] = (a * subg_ref[...] * (1.0 - lam_init)).astype(o_ref.dtype)


def _attn_sample(q, kb, vf, cache_k, cache_v, rel_bias, diff_lam, sub_g, layer, lam_init, bsz, tq):
    assert tq <= 128
    n = q.shape[0]
    past = cache_k.shape[2]
    bias = _toeplitz_bias(rel_bias, lambda d: d - past, tq, past + tq) * LOG2E
    qpos = past + jnp.arange(tq, dtype=jnp.int32)[:, None]
    kpos = jnp.arange(past + tq, dtype=jnp.int32)[None, :]
    bias = jnp.where((kpos // CHUNK) <= (qpos // CHUNK), bias, MASKED)
    bias = jnp.concatenate([bias, bias], axis=1)
    cache_k = cache_k.reshape(cache_k.shape[:3] + (D_ATT,))
    cache_v = cache_v.reshape(cache_v.shape[:3] + (D_ATT,))
    cache_spec = pl.BlockSpec((None, None, past, DIFF_VD), lambda b, h: (layer, b, 0, h))
    new_spec = pl.BlockSpec((tq, DIFF_VD), lambda b, h: (b, h))
    return pl.pallas_call(
        functools.partial(_attn_sample_body, tq, lam_init),
        grid=(bsz, N_DIFF),
        in_specs=[
            new_spec, cache_spec, new_spec, cache_spec, new_spec,
            pl.BlockSpec((None, 2 * tq, past), lambda b, h: (h, 0, 0)),
            pl.BlockSpec((None, 2 * tq, tq), lambda b, h: (h, 0, 0)),
            pl.BlockSpec((None, 4, DIFF_HD), lambda b, h: (layer, 0, 0)),
            pl.BlockSpec((None, 1, DIFF_VD), lambda b, h: (layer, 0, 0)),
        ],
        out_specs=new_spec,
        out_shape=jax.ShapeDtypeStruct((n, D_ATT), BF16),
        compiler_params=_cparams(("parallel", "parallel")),
        name="attn_sample",
    )(q, cache_k, kb, cache_v, vf, bias[:, :, :past], bias[:, :, past:], diff_lam, sub_g.reshape(-1, 1, DIFF_VD))


def _retention_body(nc, q_ref, k_ref, kw_ref, v_ref, gate_ref, s0_ref, decay_ref, wq_ref, gl_ref,
                    o_ref, snew_ref, s_ref):
    c = pl.program_id(2)

    @pl.when(c == 0)
    def _():
        s_ref[...] = s0_ref[...]

    q = q_ref[...]
    v = v_ref[...]
    s_prev = s_ref[...]
    inner = _dot_nt(q, k_ref[...]) * decay_ref[...]
    o = _dot(inner.astype(BF16), v) + _dot(q, s_prev.astype(BF16)) * wq_ref[...]
    s_new = gl_ref[...] * s_prev + _dot_tn(kw_ref[...], v)
    s_ref[...] = s_new

    @pl.when(c == nc - 1)
    def _():
        snew_ref[...] = s_new

    mu = jnp.mean(o, axis=1, keepdims=True)
    d = o - mu
    var = jnp.mean(d * d, axis=1, keepdims=True)
    o_ref[...] = (gate_ref[...] * (d * lax.rsqrt(var + LN_EPS))).astype(o_ref.dtype)


def _retention_tables(lc):
    log_g = jnp.log1p(-jnp.exp2(-5.0 - jnp.arange(N_RET, dtype=F32)))
    idx = jnp.arange(lc, dtype=F32)
    rel = idx[:, None] - idx[None, :]
    decay = jnp.where(rel >= 0, jnp.exp(jnp.maximum(rel, 0.0)[None] * log_g[:, None, None]), 0.0)
    w_k = jnp.exp((lc - 1.0 - idx)[None, :] * log_g[:, None])
    w_q = jnp.exp((idx + 1.0)[None, :] * log_g[:, None])
    g_l = jnp.exp(lc * log_g)
    widen = lambda w: jnp.repeat(w.T, RET_HD, axis=1)
    gl_tab = jnp.broadcast_to(g_l[:, None, None], (N_RET, 1, RET_HD))
    return decay, widen(w_q), widen(w_k), gl_tab


def _retention(rq, rk, rkw, rv, gate, s0, decay, wq_tab, gl_tab, bsz, t, lc):
    n = rq.shape[0]
    nc = t // lc
    tile = pl.BlockSpec((lc, RET_HD), lambda b, h, c: (b * nc + c, h))
    state = pl.BlockSpec((None, None, RET_HD, RET_HD), lambda b, h, c: (b, h, 0, 0))
    return pl.pallas_call(
        functools.partial(_retention_body, nc),
        grid=(bsz, N_RET, nc),
        in_specs=[
            tile, tile, tile, tile, tile, state,
            pl.BlockSpec((None, lc, lc), lambda b, h, c: (h, 0, 0)),
            pl.BlockSpec((lc, RET_HD), lambda b, h, c: (0, h)),
            pl.BlockSpec((None, 1, RET_HD), lambda b, h, c: (h, 0, 0)),
        ],
        out_specs=[tile, state],
        out_shape=[
            jax.ShapeDtypeStruct((n, D_RET), BF16),
            jax.ShapeDtypeStruct((bsz, N_RET, RET_HD, RET_HD), F32),
        ],
        scratch_shapes=[pltpu.VMEM((RET_HD, RET_HD), F32)],
        compiler_params=_cparams(("parallel", "parallel", "arbitrary")),
        name="retention",
    )(rq, rk, rkw, rv, gate, s0, decay, wq_tab, gl_tab)


def _out_proj_ln_body(alpha, x_ref, c_ref, a_ref, r_ref, w_ref, g_ref, b_ref, o_ref):
    e0 = D_CONV
    e1 = D_CONV + D_ATT
    mix = (_dot(c_ref[...], w_ref[0:e0, :]) + _dot(a_ref[...], w_ref[e0:e1, :])
           + _dot(r_ref[...], w_ref[e1:e1 + D_RET, :]))
    o_ref[...] = _layer_norm_rows(alpha * x_ref[...] + mix, g_ref[...], b_ref[...])


def _out_proj_ln(x, conv_out, attn_out, ret_out, w_out, g, b, layer, alpha, tm):
    n, d = x.shape
    d_mix = D_CONV + D_ATT + D_RET
    tile = lambda w: pl.BlockSpec((tm, w), lambda i: (i, 0))
    return pl.pallas_call(
        functools.partial(_out_proj_ln_body, alpha),
        grid=(n // tm,),
        in_specs=[
            tile(d), tile(D_CONV), tile(D_ATT), tile(D_RET),
            pl.BlockSpec((None, d_mix, d), lambda i: (layer, 0, 0), pipeline_mode=pl.Buffered(1)),
            pl.BlockSpec((1, d), lambda i: (0, 0)),
            pl.BlockSpec((1, d), lambda i: (0, 0)),
        ],
        out_specs=tile(d),
        out_shape=jax.ShapeDtypeStruct((n, d), F32),
        compiler_params=_cparams(("parallel",), V7X_VMEM_LIMIT),
        name="out_proj_ln",
    )(x, conv_out, attn_out, ret_out, w_out, g, b)


def _rope_tables(pos0, t):
    half = RET_HD // 2
    inv = 1.0 / (ROPE_BASE ** (jnp.arange(half, dtype=F32) / half))
    ang = (pos0 + jnp.arange(t, dtype=jnp.int32)).astype(F32)[:, None] * inv[None, :]
    cos = jnp.cos(ang)
    sin = jnp.sin(ang)
    return jnp.concatenate([cos, cos], axis=1), jnp.concatenate([-sin, sin], axis=1)


def _trunk(x, pos0, caches, params, tiles):
    (ln_g, ln_b, wg, wu, wd, w_in, w_out, conv_w, conv_b, conv_ln_g, conv_ln_b, diff_lam, diff_sub_g,
     rel_bias) = params
    bsz, t, d = x.shape
    depth = ln_g.shape[0]
    alpha = (2 * depth) ** 0.25
    tm_ffn, tm_tok, tm_out, t_att = tiles
    prompt = caches is None
    lc = tm_tok
    cos, sin = _rope_tables(pos0, t)
    decay, wq_tab, wk_tab, gl_tab = _retention_tables(lc)
    x = x.reshape(bsz * t, d)
    conv_l, k_l, v_l, s_l = [], [], [], []
    for l in range(depth):
        lam_init = 0.8 - 0.6 * math.exp(-0.3 * l)
        if prompt:
            conv_prev = jnp.zeros((bsz, CONV_WIDTH - 1, D_CONV), F32)
            ret_prev = jnp.zeros((bsz, N_RET, RET_HD, RET_HD), F32)
        else:
            cache_conv, cache_k, cache_v, state_ret = caches
            conv_prev, ret_prev = cache_conv[l], state_ret[l]
        x = _ffn_ln(x, wg, wu, wd, ln_g[l, 0:1], ln_b[l, 0:1], l, 0, alpha, tm_ffn)
        u, q, kf, kb, vf, vt, rq, rk, rkw, rv, gate = _in_proj(x, w_in, l, cos, sin, wk_tab, bsz, t, tm_tok)
        conv_out, conv_new = _conv_module(u, conv_prev, conv_w, conv_b, conv_ln_g, conv_ln_b, l, bsz, t, tm_tok)
        if prompt:
            attn_out = _attn_prompt(q, kb, vt, rel_bias, diff_lam, diff_sub_g, l, lam_init, bsz, t, t_att)
        else:
            attn_out = _attn_sample(q, kb, vf, cache_k, cache_v, rel_bias, diff_lam, diff_sub_g, l, lam_init, bsz, t)
        ret_out, s_new = _retention(rq, rk, rkw, rv, gate, ret_prev, decay, wq_tab, gl_tab, bsz, t, lc)
        x = _out_proj_ln(x, conv_out, attn_out, ret_out, w_out, ln_g[l, 1:2], ln_b[l, 1:2], l, alpha, tm_out)
        x = _ffn_ln(x, wg, wu, wd, ln_g[l, 2:3], ln_b[l, 2:3], l, 1, alpha, tm_ffn)
        conv_l.append(conv_new)
        k_l.append(kf.reshape(bsz, t, N_DIFF, DIFF_VD))
        v_l.append(vf.reshape(bsz, t, N_DIFF, DIFF_VD))
        s_l.append(s_new)
    return x.reshape(bsz, t, d), jnp.stack(conv_l), jnp.stack(k_l), jnp.stack(v_l), jnp.stack(s_l)


def _column_blocks(w, tf):
    nl, ns, d, ff = w.shape
    return jnp.transpose(w.astype(BF16).reshape(nl, ns, d, ff // tf, tf), (0, 1, 3, 2, 4))


def _pick(n, cap):
    if n <= cap:
        return n
    best = None
    for c in range(128, cap + 1, 128):
        if n % c == 0:
            best = c
    assert best is not None
    return best


def kernel(x_prompt, x_sample, cache_conv, cache_k, cache_v, state_ret, ln_g, ln_b, ffn_w_gate, ffn_w_up,
           ffn_w_down, w_in, w_out, conv_w, conv_b, conv_ln_g, conv_ln_b, diff_lam, diff_sub_g, rel_bias):
    tf = _pick(ffn_w_gate.shape[-1], 256)
    params = (ln_g, ln_b, _column_blocks(ffn_w_gate, tf), _column_blocks(ffn_w_up, tf), ffn_w_down.astype(BF16),
              w_in.astype(BF16), w_out.astype(BF16), conv_w, conv_b, conv_ln_g, conv_ln_b, diff_lam, diff_sub_g,
              rel_bias)
    bp, tp, _ = x_prompt.shape
    bs, ts, _ = x_sample.shape
    tok_p = _pick(tp, 256)
    y_p, conv_p, k_p, v_p, ret_p = _trunk(
        x_prompt, 0, None, params, (_pick(bp * tp, 1024), tok_p, _pick(bp * tp, 512), _pick(tp, 512)))
    y_s, conv_s, k_s, v_s, ret_s = _trunk(
        x_sample, cache_k.shape[2], (cache_conv, cache_k, cache_v, state_ret), params,
        (_pick(bs * ts, 512), ts, _pick(bs * ts, 512), None))
    return (y_p, y_s, conv_p, k_p, v_p, ret_p, conv_s, k_s, v_s, ret_s)
```

```python
import functools
import math

import jax
import jax.numpy as jnp
import numpy as np
from jax import lax
from jax.experimental import pallas as pl
from jax.experimental.pallas import tpu as pltpu

CHUNK = 64
CONV_WIDTH = 31
D_CONV = 512
N_DIFF = 8
DIFF_HD = 64
DIFF_VD = 2 * DIFF_HD
N_RET = 4
RET_HD = 128
N_BUCKETS = 32
MAX_DISTANCE = 128
LN_EPS = 1e-5
ROPE_BASE = 10000.0

D_ATT = N_DIFF * DIFF_VD
VT_ROWS = DIFF_VD + 16
D_RET = N_RET * RET_HD
C_CA, C_CG = 0, D_CONV
C_Q = 2 * D_CONV
C_K = C_Q + D_ATT
C_V = C_K + D_ATT
C_RQ = C_V + D_ATT
C_RK = C_RQ + D_RET
C_RV = C_RK + D_RET
C_RG = C_RV + D_RET
D_IN = C_RG + D_RET

LOG2E = math.log2(math.e)
REF_SLACK = 8.0
MASKED = -1e30
ATTN_ROW_BLOCK = 32
CONV_PAD = 32
V7X_VMEM_LIMIT = 56 * 1024 * 1024

F32 = jnp.float32
BF16 = jnp.bfloat16


def _cparams(sem, vmem=None):
    return pltpu.CompilerParams(dimension_semantics=sem, vmem_limit_bytes=vmem)


def _sigmoid(x):
    return 1.0 / (1.0 + jnp.exp(-x))


def _layer_norm_rows(y, g, b):
    mu = jnp.mean(y, axis=-1, keepdims=True)
    d = y - mu
    var = jnp.mean(d * d, axis=-1, keepdims=True)
    return d * lax.rsqrt(var + LN_EPS) * g + b


def _dot(a, b):
    return jnp.dot(a, b, preferred_element_type=F32)


def _dot_nt(a, b):
    return lax.dot_general(a, b, (((1,), (1,)), ((), ())), preferred_element_type=F32)


def _dot_tn(a, b):
    return lax.dot_general(a, b, (((0,), (0,)), ((), ())), preferred_element_type=F32)


def _ffn_ln_body(alpha, nf, x_ref, wg_ref, wu_ref, wd_ref, g_ref, b_ref, o_ref, xb_ref):
    f = pl.program_id(1)

    @pl.when(f == 0)
    def _():
        xb_ref[...] = x_ref[...].astype(BF16)
        o_ref[...] = jnp.zeros_like(o_ref)

    xb = xb_ref[...]
    hg = _dot(xb, wg_ref[...])
    hu = _dot(xb, wu_ref[...])
    act = (hg * _sigmoid(hg) * hu).astype(BF16)
    o_ref[...] += _dot(act, wd_ref[...])

    @pl.when(f == nf - 1)
    def _():
        y = alpha * x_ref[...] + 0.5 * o_ref[...]
        o_ref[...] = _layer_norm_rows(y, g_ref[...], b_ref[...])


def _ffn_ln(x, wg, wu, wd, g, b, layer, slot, alpha, tm, tf):
    n, d = x.shape
    nf = wg.shape[-1] // tf
    return pl.pallas_call(
        functools.partial(_ffn_ln_body, alpha, nf),
        grid=(n // tm, nf),
        in_specs=[
            pl.BlockSpec((tm, d), lambda i, f: (i, 0)),
            pl.BlockSpec((None, None, d, tf), lambda i, f: (layer, slot, 0, f)),
            pl.BlockSpec((None, None, d, tf), lambda i, f: (layer, slot, 0, f)),
            pl.BlockSpec((None, None, tf, d), lambda i, f: (layer, slot, f, 0)),
            pl.BlockSpec((1, d), lambda i, f: (0, 0)),
            pl.BlockSpec((1, d), lambda i, f: (0, 0)),
        ],
        out_specs=pl.BlockSpec((tm, d), lambda i, f: (i, 0)),
        out_shape=jax.ShapeDtypeStruct((n, d), F32),
        scratch_shapes=[pltpu.VMEM((tm, d), BF16)],
        compiler_params=_cparams(("parallel", "arbitrary"), V7X_VMEM_LIMIT),
        name="ffn_ln",
    )(x, wg, wu, wd, g, b)


def _rope_heads(x, cos, sin):
    outs = []
    for h in range(N_RET):
        xh = x[:, h * RET_HD:(h + 1) * RET_HD]
        outs.append(xh * cos + pltpu.roll(xh, RET_HD // 2, 1) * sin)
    return jnp.concatenate(outs, axis=1)


def _in_proj_body(tm, h_ref, w_ref, cos_ref, sin_ref, wk_ref,
                  u_ref, q_ref, kf_ref, kb_ref, vf_ref, vt_ref, rq_ref, rk_ref, rkw_ref, rv_ref, rg_ref):
    hb = h_ref[...].astype(BF16)

    def mm(lo, hi):
        return _dot(hb, w_ref[:, lo:hi])

    u_ref[...] = mm(C_CA, C_CG) * _sigmoid(mm(C_CG, C_Q))
    q_ref[...] = (mm(C_Q, C_K) * (DIFF_HD ** -0.5 * LOG2E)).astype(BF16)
    k = mm(C_K, C_V)
    kf_ref[...] = k
    kb_ref[...] = k.astype(BF16)
    v = mm(C_V, C_RQ)
    vf_ref[...] = v
    vt_ref[:, 0:DIFF_VD, :] = v.T.reshape(N_DIFF, DIFF_VD, tm).astype(BF16)
    pad_row = lax.broadcasted_iota(jnp.int32, (N_DIFF, VT_ROWS - DIFF_VD, tm), 1)
    vt_ref[:, DIFF_VD:VT_ROWS, :] = jnp.where(pad_row == 0, 1.0, 0.0).astype(BF16)
    cos = cos_ref[...]
    sin = sin_ref[...]
    rq_ref[...] = _rope_heads(mm(C_RQ, C_RK), cos, sin).astype(BF16)
    rk = _rope_heads(mm(C_RK, C_RV), cos, sin) * (RET_HD ** -0.5)
    rk_ref[...] = rk.astype(BF16)
    rkw_ref[...] = (rk * wk_ref[...]).astype(BF16)
    rv_ref[...] = mm(C_RV, C_RG).astype(BF16)
    rg = mm(C_RG, D_IN)
    rg_ref[...] = rg * _sigmoid(rg)


def _in_proj(h, w_in, layer, cos, sin, wk_tab, bsz, t, tm):
    n, d = h.shape
    nt = t // tm
    row = lambda b, i: (b * nt + i, 0)
    outs = [
        ((n, D_CONV), F32),
        ((n, D_ATT), BF16),
        ((n, D_ATT), F32),
        ((n, D_ATT), BF16),
        ((n, D_ATT), F32),
        None,
        ((n, D_RET), BF16),
        ((n, D_RET), BF16),
        ((n, D_RET), BF16),
        ((n, D_RET), BF16),
        ((n, D_RET), F32),
    ]
    out_shape, out_specs = [], []
    for o in outs:
        if o is None:
            out_shape.append(jax.ShapeDtypeStruct((bsz, N_DIFF, nt, VT_ROWS, tm), BF16))
            out_specs.append(pl.BlockSpec((None, N_DIFF, None, VT_ROWS, tm), lambda b, i: (b, 0, i, 0, 0)))
        else:
            out_shape.append(jax.ShapeDtypeStruct(*o))
            out_specs.append(pl.BlockSpec((tm, o[0][1]), row))
    return pl.pallas_call(
        functools.partial(_in_proj_body, tm),
        grid=(bsz, nt),
        in_specs=[
            pl.BlockSpec((tm, d), row),
            pl.BlockSpec((None, d, D_IN), lambda b, i: (layer, 0, 0), pipeline_mode=pl.Buffered(1)),
            pl.BlockSpec((tm, RET_HD), lambda b, i: (i, 0)),
            pl.BlockSpec((tm, RET_HD), lambda b, i: (i, 0)),
            pl.BlockSpec((tm, D_RET), lambda b, i: (0, 0)),
        ],
        out_specs=out_specs,
        out_shape=out_shape,
        compiler_params=_cparams(("parallel", "parallel"), V7X_VMEM_LIMIT),
        name="in_proj",
    )(h, w_in, cos, sin, wk_tab)


def _conv_body(tm, nt, u_ref, prev_ref, w_ref, cb_ref, g_ref, b_ref, o_ref, new_ref, ext_ref):
    i = pl.program_id(1)
    hist = CONV_WIDTH - 1

    @pl.when(i == 0)
    def _():
        ext_ref[0:CONV_PAD - hist, :] = jnp.zeros((CONV_PAD - hist, D_CONV), F32)
        ext_ref[CONV_PAD - hist:CONV_PAD, :] = prev_ref[...]

    @pl.when(i > 0)
    def _():
        ext_ref[0:CONV_PAD, :] = ext_ref[tm:tm + CONV_PAD, :]

    ext_ref[CONV_PAD:CONV_PAD + tm, :] = u_ref[...]

    rb = 32
    cb = cb_ref[...]
    g = g_ref[...]
    b = b_ref[...]
    for r0 in range(0, tm, rb):
        acc = jnp.zeros((rb, D_CONV), F32)
        for w in range(CONV_WIDTH):
            start = CONV_PAD - hist + w + r0
            acc = acc + ext_ref[start:start + rb, :] * w_ref[w:w + 1, :]
        y = _layer_norm_rows(acc + cb, g, b)
        o_ref[r0:r0 + rb, :] = (y * _sigmoid(y)).astype(o_ref.dtype)

    @pl.when(i == nt - 1)
    def _():
        new_ref[...] = ext_ref[CONV_PAD + tm - hist:CONV_PAD + tm, :]


def _conv_module(u, prev, conv_w, conv_b, g, b, layer, bsz, t, tm):
    n = u.shape[0]
    nt = t // tm
    assert t >= CONV_PAD and tm >= CONV_PAD and tm % 32 == 0
    vec = lambda: pl.BlockSpec((None, 1, D_CONV), lambda bb, i: (layer, 0, 0))
    return pl.pallas_call(
        functools.partial(_conv_body, tm, nt),
        grid=(bsz, nt),
        in_specs=[
            pl.BlockSpec((tm, D_CONV), lambda bb, i: (bb * nt + i, 0)),
            pl.BlockSpec((None, CONV_WIDTH - 1, D_CONV), lambda bb, i: (bb, 0, 0)),
            pl.BlockSpec((None, CONV_WIDTH, D_CONV), lambda bb, i: (layer, 0, 0)),
            vec(), vec(), vec(),
        ],
        out_specs=[
            pl.BlockSpec((tm, D_CONV), lambda bb, i: (bb * nt + i, 0)),
            pl.BlockSpec((None, CONV_WIDTH - 1, D_CONV), lambda bb, i: (bb, 0, 0)),
        ],
        out_shape=[
            jax.ShapeDtypeStruct((n, D_CONV), BF16),
            jax.ShapeDtypeStruct((bsz, CONV_WIDTH - 1, D_CONV), F32),
        ],
        scratch_shapes=[pltpu.VMEM((tm + CONV_PAD, D_CONV), F32)],
        compiler_params=_cparams(("parallel", "arbitrary")),
        name="conv_module",
    )(u, prev, conv_w, conv_b.reshape(-1, 1, D_CONV), g.reshape(-1, 1, D_CONV), b.reshape(-1, 1, D_CONV))


def _rel_bucket(rel):
    half = N_BUCKETS // 2
    max_exact = half // 2
    ret = jnp.where(rel > 0, half, 0)
    n = jnp.abs(rel)
    nf = jnp.maximum(n, 1).astype(F32)
    large = max_exact + (jnp.log(nf / max_exact) / math.log(MAX_DISTANCE / max_exact)
                         * (half - max_exact)).astype(jnp.int32)
    large = jnp.minimum(large, half - 1)
    return ret + jnp.where(n < max_exact, n, large)


def _toeplitz_bias(rel_bias, rel_of_diff, rows, cols):
    period = rows + cols
    x = np.arange(period)
    diff = np.where(x < cols, x, x - period)
    rel = jnp.asarray(rel_of_diff(diff), jnp.int32)
    wp = rel_bias[_rel_bucket(rel)].astype(F32).T
    flat = jnp.tile(wp, (1, rows))[:, :rows * (period - 1)]
    return flat.reshape(-1, rows, period - 1)[:, :, :cols]


def _lam_value(lam_ref, lam_init):
    lf = lam_ref[...]
    a = jnp.sum(lf[0:1, :] * lf[1:2, :], axis=1, keepdims=True)
    b = jnp.sum(lf[2:3, :] * lf[3:4, :], axis=1, keepdims=True)
    return jnp.exp(a) - jnp.exp(b) + lam_init


def _block_diag_q(q):
    lane = lax.broadcasted_iota(jnp.int32, q.shape, 1)
    zero = jnp.zeros_like(q)
    return jnp.concatenate([jnp.where(lane < DIFF_HD, q, zero), jnp.where(lane >= DIFF_HD, q, zero)], axis=0)


def _attn_prompt_body(t, tv, nq, lam_init, q_ref, k_ref, vt_ref, bias_ref, lam_ref, subg_ref, o_ref,
                      qbd_ref, pa_ref, pb_ref, r_ref, g_ref, acc_ref):
    i = pl.program_id(2)
    sub = t // tv
    rb = ATTN_ROW_BLOCK
    lanes = 2 * t
    q_t = q_ref[...].astype(F32).T.astype(BF16)
    head_row = lax.broadcasted_iota(jnp.int32, q_t.shape, 0)
    zero = jnp.zeros_like(q_t)
    qbd_ref[:, 0:t] = jnp.where(head_row < DIFF_HD, q_t, zero)
    qbd_ref[:, t:lanes] = jnp.where(head_row >= DIFF_HD, q_t, zero)

    def tile_scores(j, bias_idx):
        k = k_ref[pl.ds(pl.multiple_of(j * t, t), t), :]
        s = _dot(k, qbd_ref[...])
        if bias_idx is not None:
            bias = bias_ref[bias_idx]
            s = s + jnp.concatenate([bias, bias], axis=1)
        return s

    def column_max(s):
        return jnp.max(s.reshape(t // 8, 8, lanes), axis=0)

    def step(j, bias_idx, p_ref):
        s = tile_scores(j, bias_idx)
        g_ref[...] = jnp.maximum(g_ref[...], column_max(s))
        r = r_ref[...]
        for c in range(t // rb):
            p_ref[c * rb:(c + 1) * rb, :] = jnp.exp2((s[c * rb:(c + 1) * rb, :] - r).astype(BF16))
        pv = None
        for u in range(sub):
            part = _dot(vt_ref[j * sub + u], p_ref[u * tv:(u + 1) * tv, :])
            pv = part if pv is None else pv + part
        acc_ref[...] += pv

    def sweep():
        acc_ref[...] = jnp.zeros_like(acc_ref)
        g_ref[...] = jnp.full(g_ref.shape, MASKED, F32)

        @pl.when(i == 0)
        def _():
            step(i, 0, pa_ref)

        @pl.when(i >= 1)
        def _():
            step(i, 0, pa_ref)
            step(i - 1, 1, pb_ref)

        n_far = jnp.maximum(i - 1, 0)
        odd = n_far % 2

        @pl.when(odd == 1)
        def _():
            step(i - 2, None, pa_ref)

        def far_pair(c, carry):
            j = i - 2 - odd - 2 * c
            step(j, None, pa_ref)
            step(j - 1, None, pb_ref)
            return carry

        lax.fori_loop(0, n_far // 2, far_pair, 0)

    r_ref[...] = jnp.max(column_max(tile_scores(i, 0)), axis=0, keepdims=True)
    sweep()
    true_max = jnp.max(g_ref[...], axis=0, keepdims=True)

    @pl.when(jnp.max(true_max - r_ref[...]) > REF_SLACK)
    def _():
        r_ref[...] = true_max
        sweep()

    acc = acc_ref[...]
    o = acc[0:DIFF_VD, :] / acc[DIFF_VD:DIFF_VD + 1, :]
    lam = _lam_value(lam_ref, lam_init)
    a = o[:, :t] - lam * o[:, t:]
    a = a * lax.rsqrt(jnp.mean(a * a, axis=0, keepdims=True) + LN_EPS)
    o_ref[...] = (a.T * subg_ref[...] * (1.0 - lam_init)).astype(o_ref.dtype)


def _attn_prompt(q, kb, vt, rel_bias, diff_lam, sub_g, layer, lam_init, bsz, seq, t):
    tv = vt.shape[-1]
    assert t >= MAX_DISTANCE and t % CHUNK == 0 and seq % t == 0 and t % tv == 0
    n = q.shape[0]
    nq = seq // t
    far_bias = rel_bias[_rel_bucket(jnp.full((), -MAX_DISTANCE, jnp.int32))].astype(F32)[:, None, None]
    b0 = _toeplitz_bias(rel_bias, lambda d: -d, t, t) - far_bias
    b1 = _toeplitz_bias(rel_bias, lambda d: -d - t, t, t) - far_bias
    r = jnp.arange(t, dtype=jnp.int32)[:, None]
    c = jnp.arange(t, dtype=jnp.int32)[None, :]
    b0 = jnp.where((r // CHUNK) <= (c // CHUNK), b0 * LOG2E, MASKED)
    tiles = jnp.stack([b0, b1 * LOG2E], axis=1)

    return pl.pallas_call(
        functools.partial(_attn_prompt_body, t, tv, nq, lam_init),
        grid=(bsz, N_DIFF, nq),
        in_specs=[
            pl.BlockSpec((t, DIFF_VD), lambda b, h, i: (b * nq + i, h)),
            pl.BlockSpec((seq, DIFF_VD), lambda b, h, i: (b, h)),
            pl.BlockSpec((None, None, seq // tv, VT_ROWS, tv), lambda b, h, i: (b, h, 0, 0, 0)),
            pl.BlockSpec((None, 2, t, t), lambda b, h, i: (h, 0, 0, 0), pipeline_mode=pl.Buffered(1)),
            pl.BlockSpec((None, 4, DIFF_HD), lambda b, h, i: (layer, 0, 0)),
            pl.BlockSpec((None, 1, DIFF_VD), lambda b, h, i: (layer, 0, 0)),
        ],
        out_specs=pl.BlockSpec((t, DIFF_VD), lambda b, h, i: (b * nq + i, h)),
        out_shape=jax.ShapeDtypeStruct((n, D_ATT), BF16),
        scratch_shapes=[
            pltpu.VMEM((DIFF_VD, 2 * t), BF16),
            pltpu.VMEM((t, 2 * t), BF16),
            pltpu.VMEM((t, 2 * t), BF16),
            pltpu.VMEM((1, 2 * t), F32),
            pltpu.VMEM((8, 2 * t), F32),
            pltpu.VMEM((VT_ROWS, 2 * t), F32),
        ],
        compiler_params=_cparams(("parallel", "parallel", "arbitrary"), V7X_VMEM_LIMIT),
        name="attn_prompt",
    )(q, kb, vt, tiles, diff_lam, sub_g.reshape(-1, 1, DIFF_VD))


def _attn_sample_body(tq, lam_init, q_ref, kp_ref, kn_ref, vp_ref, vn_ref, bp_ref, bn_ref, lam_ref, subg_ref, o_ref):
    qbd = _block_diag_q(q_ref[...])
    s_p = _dot_nt(qbd, kp_ref[...].astype(BF16)) + bp_ref[...]
    s_n = _dot_nt(qbd, kn_ref[...]) + bn_ref[...]
    m = jnp.maximum(jnp.max(s_p, axis=1, keepdims=True), jnp.max(s_n, axis=1, keepdims=True))
    p_p = jnp.exp2(s_p - m)
    p_n = jnp.exp2(s_n - m)
    l = jnp.sum(p_p, axis=1, keepdims=True) + jnp.sum(p_n, axis=1, keepdims=True)
    o = (_dot(p_p.astype(BF16), vp_ref[...].astype(BF16)) + _dot(p_n.astype(BF16), vn_ref[...].astype(BF16))) / l
    lam = _lam_value(lam_ref, lam_init)
    a = o[:tq, :] - lam * o[tq:, :]
    a = a * lax.rsqrt(jnp.mean(a * a, axis=1, keepdims=True) + LN_EPS)
    o_ref[...] = (a * subg_ref[...] * (1.0 - lam_init)).astype(o_ref.dtype)


def _attn_sample(q, kb, vf, cache_k, cache_v, rel_bias, diff_lam, sub_g, layer, lam_init, bsz, tq):
    assert tq <= 128
    n = q.shape[0]
    past = cache_k.shape[2]
    bias = _toeplitz_bias(rel_bias, lambda d: d - past, tq, past + tq) * LOG2E
    qpos = past + jnp.arange(tq, dtype=jnp.int32)[:, None]
    kpos = jnp.arange(past + tq, dtype=jnp.int32)[None, :]
    bias = jnp.where((kpos // CHUNK) <= (qpos // CHUNK), bias, MASKED)
    bias = jnp.concatenate([bias, bias], axis=1)
    cache_k = cache_k.reshape(cache_k.shape[:3] + (D_ATT,))
    cache_v = cache_v.reshape(cache_v.shape[:3] + (D_ATT,))
    cache_spec = pl.BlockSpec((None, None, past, DIFF_VD), lambda b, h: (layer, b, 0, h))
    new_spec = pl.BlockSpec((tq, DIFF_VD), lambda b, h: (b, h))
    return pl.pallas_call(
        functools.partial(_attn_sample_body, tq, lam_init),
        grid=(bsz, N_DIFF),
        in_specs=[
            new_spec, cache_spec, new_spec, cache_spec, new_spec,
            pl.BlockSpec((None, 2 * tq, past), lambda b, h: (h, 0, 0)),
            pl.BlockSpec((None, 2 * tq, tq), lambda b, h: (h, 0, 0)),
            pl.BlockSpec((None, 4, DIFF_HD), lambda b, h: (layer, 0, 0)),
            pl.BlockSpec((None, 1, DIFF_VD), lambda b, h: (layer, 0, 0)),
        ],
        out_specs=new_spec,
        out_shape=jax.ShapeDtypeStruct((n, D_ATT), BF16),
        compiler_params=_cparams(("parallel", "parallel")),
        name="attn_sample",
    )(q, cache_k, kb, cache_v, vf, bias[:, :, :past], bias[:, :, past:], diff_lam, sub_g.reshape(-1, 1, DIFF_VD))


def _retention_body(nc, q_ref, k_ref, kw_ref, v_ref, gate_ref, s0_ref, decay_ref, wq_ref, gl_ref,
                    o_ref, snew_ref, s_ref):
    c = pl.program_id(2)

    @pl.when(c == 0)
    def _():
        s_ref[...] = s0_ref[...]

    q = q_ref[...]
    v = v_ref[...]
    s_prev = s_ref[...]
    inner = _dot_nt(q, k_ref[...]) * decay_ref[...]
    o = _dot(inner.astype(BF16), v) + _dot(q, s_prev.astype(BF16)) * wq_ref[...]
    s_new = gl_ref[...] * s_prev + _dot_tn(kw_ref[...], v)
    s_ref[...] = s_new

    @pl.when(c == nc - 1)
    def _():
        snew_ref[...] = s_new

    mu = jnp.mean(o, axis=1, keepdims=True)
    d = o - mu
    var = jnp.mean(d * d, axis=1, keepdims=True)
    o_ref[...] = (gate_ref[...] * (d * lax.rsqrt(var + LN_EPS))).astype(o_ref.dtype)


def _retention_tables(lc):
    log_g = jnp.log1p(-jnp.exp2(-5.0 - jnp.arange(N_RET, dtype=F32)))
    idx = jnp.arange(lc, dtype=F32)
    rel = idx[:, None] - idx[None, :]
    decay = jnp.where(rel >= 0, jnp.exp(jnp.maximum(rel, 0.0)[None] * log_g[:, None, None]), 0.0)
    w_k = jnp.exp((lc - 1.0 - idx)[None, :] * log_g[:, None])
    w_q = jnp.exp((idx + 1.0)[None, :] * log_g[:, None])
    g_l = jnp.exp(lc * log_g)
    widen = lambda w: jnp.repeat(w.T, RET_HD, axis=1)
    gl_tab = jnp.broadcast_to(g_l[:, None, None], (N_RET, 1, RET_HD))
    return decay, widen(w_q), widen(w_k), gl_tab


def _retention(rq, rk, rkw, rv, gate, s0, decay, wq_tab, gl_tab, bsz, t, lc):
    n = rq.shape[0]
    nc = t // lc
    tile = pl.BlockSpec((lc, RET_HD), lambda b, h, c: (b * nc + c, h))
    state = pl.BlockSpec((None, None, RET_HD, RET_HD), lambda b, h, c: (b, h, 0, 0))
    return pl.pallas_call(
        functools.partial(_retention_body, nc),
        grid=(bsz, N_RET, nc),
        in_specs=[
            tile, tile, tile, tile, tile, state,
            pl.BlockSpec((None, lc, lc), lambda b, h, c: (h, 0, 0)),
            pl.BlockSpec((lc, RET_HD), lambda b, h, c: (0, h)),
            pl.BlockSpec((None, 1, RET_HD), lambda b, h, c: (h, 0, 0)),
        ],
        out_specs=[tile, state],
        out_shape=[
            jax.ShapeDtypeStruct((n, D_RET), BF16),
            jax.ShapeDtypeStruct((bsz, N_RET, RET_HD, RET_HD), F32),
        ],
        scratch_shapes=[pltpu.VMEM((RET_HD, RET_HD), F32)],
        compiler_params=_cparams(("parallel", "parallel", "arbitrary")),
        name="retention",
    )(rq, rk, rkw, rv, gate, s0, decay, wq_tab, gl_tab)


def _out_proj_ln_body(alpha, x_ref, c_ref, a_ref, r_ref, w_ref, g_ref, b_ref, o_ref):
    e0 = D_CONV
    e1 = D_CONV + D_ATT
    mix = (_dot(c_ref[...], w_ref[0:e0, :]) + _dot(a_ref[...], w_ref[e0:e1, :])
           + _dot(r_ref[...], w_ref[e1:e1 + D_RET, :]))
    o_ref[...] = _layer_norm_rows(alpha * x_ref[...] + mix, g_ref[...], b_ref[...])


def _out_proj_ln(x, conv_out, attn_out, ret_out, w_out, g, b, layer, alpha, tm):
    n, d = x.shape
    d_mix = D_CONV + D_ATT + D_RET
    tile = lambda w: pl.BlockSpec((tm, w), lambda i: (i, 0))
    return pl.pallas_call(
        functools.partial(_out_proj_ln_body, alpha),
        grid=(n // tm,),
        in_specs=[
            tile(d), tile(D_CONV), tile(D_ATT), tile(D_RET),
            pl.BlockSpec((None, d_mix, d), lambda i: (layer, 0, 0), pipeline_mode=pl.Buffered(1)),
            pl.BlockSpec((1, d), lambda i: (0, 0)),
            pl.BlockSpec((1, d), lambda i: (0, 0)),
        ],
        out_specs=tile(d),
        out_shape=jax.ShapeDtypeStruct((n, d), F32),
        compiler_params=_cparams(("parallel",), V7X_VMEM_LIMIT),
        name="out_proj_ln",
    )(x, conv_out, attn_out, ret_out, w_out, g, b)


def _rope_tables(pos0, t):
    half = RET_HD // 2
    inv = 1.0 / (ROPE_BASE ** (jnp.arange(half, dtype=F32) / half))
    ang = (pos0 + jnp.arange(t, dtype=jnp.int32)).astype(F32)[:, None] * inv[None, :]
    cos = jnp.cos(ang)
    sin = jnp.sin(ang)
    return jnp.concatenate([cos, cos], axis=1), jnp.concatenate([-sin, sin], axis=1)


def _trunk(x, pos0, caches, params, tiles):
    (ln_g, ln_b, wg, wu, wd, w_in, w_out, conv_w, conv_b, conv_ln_g, conv_ln_b, diff_lam, diff_sub_g,
     rel_bias) = params
    bsz, t, d = x.shape
    depth = ln_g.shape[0]
    alpha = (2 * depth) ** 0.25
    tm_ffn, tf, tm_tok, tm_out, t_att = tiles
    prompt = caches is None
    lc = tm_tok
    cos, sin = _rope_tables(pos0, t)
    decay, wq_tab, wk_tab, gl_tab = _retention_tables(lc)
    x = x.reshape(bsz * t, d)
    conv_l, k_l, v_l, s_l = [], [], [], []
    for l in range(depth):
        lam_init = 0.8 - 0.6 * math.exp(-0.3 * l)
        if prompt:
            conv_prev = jnp.zeros((bsz, CONV_WIDTH - 1, D_CONV), F32)
            ret_prev = jnp.zeros((bsz, N_RET, RET_HD, RET_HD), F32)
        else:
            cache_conv, cache_k, cache_v, state_ret = caches
            conv_prev, ret_prev = cache_conv[l], state_ret[l]
        x = _ffn_ln(x, wg, wu, wd, ln_g[l, 0:1], ln_b[l, 0:1], l, 0, alpha, tm_ffn, tf)
        u, q, kf, kb, vf, vt, rq, rk, rkw, rv, gate = _in_proj(x, w_in, l, cos, sin, wk_tab, bsz, t, tm_tok)
        conv_out, conv_new = _conv_module(u, conv_prev, conv_w, conv_b, conv_ln_g, conv_ln_b, l, bsz, t, tm_tok)
        if prompt:
            attn_out = _attn_prompt(q, kb, vt, rel_bias, diff_lam, diff_sub_g, l, lam_init, bsz, t, t_att)
        else:
            attn_out = _attn_sample(q, kb, vf, cache_k, cache_v, rel_bias, diff_lam, diff_sub_g, l, lam_init, bsz, t)
        ret_out, s_new = _retention(rq, rk, rkw, rv, gate, ret_prev, decay, wq_tab, gl_tab, bsz, t, lc)
        x = _out_proj_ln(x, conv_out, attn_out, ret_out, w_out, ln_g[l, 1:2], ln_b[l, 1:2], l, alpha, tm_out)
        x = _ffn_ln(x, wg, wu, wd, ln_g[l, 2:3], ln_b[l, 2:3], l, 1, alpha, tm_ffn, tf)
        conv_l.append(conv_new)
        k_l.append(kf.reshape(bsz, t, N_DIFF, DIFF_VD))
        v_l.append(vf.reshape(bsz, t, N_DIFF, DIFF_VD))
        s_l.append(s_new)
    return x.reshape(bsz, t, d), jnp.stack(conv_l), jnp.stack(k_l), jnp.stack(v_l), jnp.stack(s_l)


def _pick(n, cap):
    if n <= cap:
        return n
    best = None
    for c in range(128, cap + 1, 128):
        if n % c == 0:
            best = c
    assert best is not None
    return best


def kernel(x_prompt, x_sample, cache_conv, cache_k, cache_v, state_ret, ln_g, ln_b, ffn_w_gate, ffn_w_up,
           ffn_w_down, w_in, w_out, conv_w, conv_b, conv_ln_g, conv_ln_b, diff_lam, diff_sub_g, rel_bias):
    tf = _pick(ffn_w_gate.shape[-1], 512)
    params = (ln_g, ln_b, ffn_w_gate.astype(BF16), ffn_w_up.astype(BF16), ffn_w_down.astype(BF16),
              w_in.astype(BF16), w_out.astype(BF16), conv_w, conv_b, conv_ln_g, conv_ln_b, diff_lam, diff_sub_g,
              rel_bias)
    bp, tp, _ = x_prompt.shape
    bs, ts, _ = x_sample.shape
    tok_p = _pick(tp, 256)
    y_p, conv_p, k_p, v_p, ret_p = _trunk(
        x_prompt, 0, None, params, (_pick(bp * tp, 512), tf, tok_p, _pick(bp * tp, 512), _pick(tp, 512)))
    y_s, conv_s, k_s, v_s, ret_s = _trunk(
        x_sample, cache_k.shape[2], (cache_conv, cache_k, cache_v, state_ret), params,
        (_pick(bs * ts, 512), tf, ts, _pick(bs * ts, 512), None))
    return (y_p, y_s, conv_p, k_p, v_p, ret_p, conv_s, k_s, v_s, ret_s)
```

```python
import functools
import math

import jax
import jax.numpy as jnp
import numpy as np
from jax import lax
from jax.experimental import pallas as pl
from jax.experimental.pallas import tpu as pltpu

CHUNK = 64
CONV_WIDTH = 31
D_CONV = 512
N_DIFF = 8
DIFF_HD = 64
DIFF_VD = 2 * DIFF_HD
N_RET = 4
RET_HD = 128
N_BUCKETS = 32
MAX_DISTANCE = 128
LN_EPS = 1e-5
ROPE_BASE = 10000.0

D_ATT = N_DIFF * DIFF_VD
VT_ROWS = DIFF_VD + 16
D_RET = N_RET * RET_HD
C_CA, C_CG = 0, D_CONV
C_Q = 2 * D_CONV
C_K = C_Q + D_ATT
C_V = C_K + D_ATT
C_RQ = C_V + D_ATT
C_RK = C_RQ + D_RET
C_RV = C_RK + D_RET
C_RG = C_RV + D_RET
D_IN = C_RG + D_RET

LOG2E = math.log2(math.e)
REF_SLACK = 8.0
MASKED = -1e30
ATTN_ROW_BLOCK = 32
SUBLANES = 8
CONV_PAD = 32
V7X_VMEM_LIMIT = 56 * 1024 * 1024

F32 = jnp.float32
BF16 = jnp.bfloat16


def _cparams(sem, vmem=None):
    return pltpu.CompilerParams(dimension_semantics=sem, vmem_limit_bytes=vmem)


def _sigmoid(x):
    return 1.0 / (1.0 + jnp.exp(-x))


def _layer_norm_rows(y, g, b):
    mu = jnp.mean(y, axis=-1, keepdims=True)
    d = y - mu
    var = jnp.mean(d * d, axis=-1, keepdims=True)
    return d * lax.rsqrt(var + LN_EPS) * g + b


def _dot(a, b):
    return jnp.dot(a, b, preferred_element_type=F32)


def _dot_nt(a, b):
    return lax.dot_general(a, b, (((1,), (1,)), ((), ())), preferred_element_type=F32)


def _dot_tn(a, b):
    return lax.dot_general(a, b, (((0,), (0,)), ((), ())), preferred_element_type=F32)


def _ffn_ln_body(alpha, nf, x_ref, wg_ref, wu_ref, wd_ref, g_ref, b_ref, o_ref, xb_ref):
    f = pl.program_id(1)

    @pl.when(f == 0)
    def _():
        xb_ref[...] = x_ref[...].astype(BF16)
        o_ref[...] = jnp.zeros_like(o_ref)

    xb = xb_ref[...]
    hg = _dot(xb, wg_ref[...])
    hu = _dot(xb, wu_ref[...])
    act = (hg * _sigmoid(hg) * hu).astype(BF16)
    o_ref[...] += _dot(act, wd_ref[...])

    @pl.when(f == nf - 1)
    def _():
        y = alpha * x_ref[...] + 0.5 * o_ref[...]
        o_ref[...] = _layer_norm_rows(y, g_ref[...], b_ref[...])


def _ffn_ln(x, wg, wu, wd, g, b, layer, slot, alpha, tm, tf):
    n, d = x.shape
    nf = wg.shape[-1] // tf
    return pl.pallas_call(
        functools.partial(_ffn_ln_body, alpha, nf),
        grid=(n // tm, nf),
        in_specs=[
            pl.BlockSpec((tm, d), lambda i, f: (i, 0)),
            pl.BlockSpec((None, None, d, tf), lambda i, f: (layer, slot, 0, f)),
            pl.BlockSpec((None, None, d, tf), lambda i, f: (layer, slot, 0, f)),
            pl.BlockSpec((None, None, tf, d), lambda i, f: (layer, slot, f, 0)),
            pl.BlockSpec((1, d), lambda i, f: (0, 0)),
            pl.BlockSpec((1, d), lambda i, f: (0, 0)),
        ],
        out_specs=pl.BlockSpec((tm, d), lambda i, f: (i, 0)),
        out_shape=jax.ShapeDtypeStruct((n, d), F32),
        scratch_shapes=[pltpu.VMEM((tm, d), BF16)],
        compiler_params=_cparams(("parallel", "arbitrary"), V7X_VMEM_LIMIT),
        name="ffn_ln",
    )(x, wg, wu, wd, g, b)


def _rope_heads(x, cos, sin):
    outs = []
    for h in range(N_RET):
        xh = x[:, h * RET_HD:(h + 1) * RET_HD]
        outs.append(xh * cos + pltpu.roll(xh, RET_HD // 2, 1) * sin)
    return jnp.concatenate(outs, axis=1)


def _in_proj_body(tm, n_aliased, h_ref, w_ref, cos_ref, sin_ref, wk_ref, *refs):
    u_ref, q_ref, kf_ref, kb_ref, vf_ref, vt_ref, rq_ref, rk_ref, rkw_ref, rv_ref, rg_ref = refs[n_aliased:]
    hb = h_ref[...].astype(BF16)

    def mm(lo, hi):
        return _dot(hb, w_ref[:, lo:hi])

    u_ref[...] = mm(C_CA, C_CG) * _sigmoid(mm(C_CG, C_Q))
    q_ref[...] = (mm(C_Q, C_K) * (DIFF_HD ** -0.5 * LOG2E)).astype(BF16)
    k = mm(C_K, C_V)
    kf_ref[...] = k
    kb_ref[...] = k.astype(BF16)
    v = mm(C_V, C_RQ)
    vf_ref[...] = v
    vt_ref[:, 0:DIFF_VD, :] = v.T.reshape(N_DIFF, DIFF_VD, tm).astype(BF16)
    pad_row = lax.broadcasted_iota(jnp.int32, (N_DIFF, VT_ROWS - DIFF_VD, tm), 1)
    vt_ref[:, DIFF_VD:VT_ROWS, :] = jnp.where(pad_row == 0, 1.0, 0.0).astype(BF16)
    cos = cos_ref[...]
    sin = sin_ref[...]
    rq_ref[...] = _rope_heads(mm(C_RQ, C_RK), cos, sin).astype(BF16)
    rk = _rope_heads(mm(C_RK, C_RV), cos, sin) * (RET_HD ** -0.5)
    rk_ref[...] = rk.astype(BF16)
    rkw_ref[...] = (rk * wk_ref[...]).astype(BF16)
    rv_ref[...] = mm(C_RV, C_RG).astype(BF16)
    rg = mm(C_RG, D_IN)
    rg_ref[...] = rg * _sigmoid(rg)


def _in_proj(h, w_in, layer, depth, kv_stacks, cos, sin, wk_tab, bsz, t, tm):
    n, d = h.shape
    nt = t // tm
    row = lambda b, i: (b * nt + i, 0)
    STACK, VT = "stack", "vt"
    outs = [
        ((n, D_CONV), F32),
        ((n, D_ATT), BF16),
        STACK,
        ((n, D_ATT), BF16),
        STACK,
        VT,
        ((n, D_RET), BF16),
        ((n, D_RET), BF16),
        ((n, D_RET), BF16),
        ((n, D_RET), BF16),
        ((n, D_RET), F32),
    ]
    out_shape, out_specs = [], []
    for o in outs:
        if o is VT:
            out_shape.append(jax.ShapeDtypeStruct((bsz, N_DIFF, nt, VT_ROWS, tm), BF16))
            out_specs.append(pl.BlockSpec((None, N_DIFF, None, VT_ROWS, tm), lambda b, i: (b, 0, i, 0, 0)))
        elif o is STACK:
            out_shape.append(jax.ShapeDtypeStruct((depth, n, D_ATT), F32))
            out_specs.append(pl.BlockSpec((None, tm, D_ATT), lambda b, i: (layer, b * nt + i, 0)))
        else:
            out_shape.append(jax.ShapeDtypeStruct(*o))
            out_specs.append(pl.BlockSpec((tm, o[0][1]), row))
    in_specs = [
        pl.BlockSpec((tm, d), row),
        pl.BlockSpec((None, d, D_IN), lambda b, i: (layer, 0, 0), pipeline_mode=pl.Buffered(1)),
        pl.BlockSpec((tm, RET_HD), lambda b, i: (i, 0)),
        pl.BlockSpec((tm, RET_HD), lambda b, i: (i, 0)),
        pl.BlockSpec((tm, D_RET), lambda b, i: (0, 0)),
    ]
    operands = [h, w_in, cos, sin, wk_tab]
    aliases = {}
    if kv_stacks is not None:
        for stack, out_idx in zip(kv_stacks, [k for k, o in enumerate(outs) if o is STACK]):
            aliases[len(operands)] = out_idx
            in_specs.append(pl.BlockSpec(memory_space=pl.ANY))
            operands.append(stack)
    return pl.pallas_call(
        functools.partial(_in_proj_body, tm, len(aliases)),
        grid=(bsz, nt),
        in_specs=in_specs,
        out_specs=out_specs,
        out_shape=out_shape,
        input_output_aliases=aliases,
        compiler_params=_cparams(("parallel", "parallel"), V7X_VMEM_LIMIT),
        name="in_proj",
    )(*operands)


def _conv_body(tm, nt, u_ref, prev_ref, w_ref, cb_ref, g_ref, b_ref, o_ref, new_ref, ext_ref):
    i = pl.program_id(1)
    hist = CONV_WIDTH - 1

    @pl.when(i == 0)
    def _():
        ext_ref[0, 0:CONV_PAD - hist, :] = jnp.zeros((CONV_PAD - hist, D_CONV), F32)
        ext_ref[0, CONV_PAD - hist:CONV_PAD, :] = prev_ref[...]

    @pl.when(i > 0)
    def _():
        ext_ref[0, 0:CONV_PAD, :] = ext_ref[0, tm:tm + CONV_PAD, :]

    ext_ref[0, CONV_PAD:CONV_PAD + tm, :] = u_ref[...]
    shifted_rows = tm + CONV_PAD - SUBLANES
    for s in range(1, SUBLANES):
        ext_ref[s, 0:shifted_rows, :] = ext_ref[0, s:s + shifted_rows, :]

    rb = 32
    cb = cb_ref[...]
    g = g_ref[...]
    b = b_ref[...]
    for r0 in range(0, tm, rb):
        acc = jnp.zeros((rb, D_CONV), F32)
        for w in range(CONV_WIDTH):
            shift = (CONV_PAD - hist + w) % SUBLANES
            start = CONV_PAD - hist + w - shift + r0
            acc = acc + ext_ref[shift, start:start + rb, :] * w_ref[w:w + 1, :]
        y = _layer_norm_rows(acc + cb, g, b)
        o_ref[r0:r0 + rb, :] = (y * _sigmoid(y)).astype(o_ref.dtype)

    @pl.when(i == nt - 1)
    def _():
        new_ref[...] = ext_ref[0, CONV_PAD + tm - hist:CONV_PAD + tm, :]


def _conv_module(u, prev, conv_w, conv_b, g, b, layer, bsz, t, tm):
    n = u.shape[0]
    nt = t // tm
    assert t >= CONV_PAD and tm >= CONV_PAD and tm % 32 == 0
    vec = lambda: pl.BlockSpec((None, 1, D_CONV), lambda bb, i: (layer, 0, 0))
    return pl.pallas_call(
        functools.partial(_conv_body, tm, nt),
        grid=(bsz, nt),
        in_specs=[
            pl.BlockSpec((tm, D_CONV), lambda bb, i: (bb * nt + i, 0)),
            pl.BlockSpec((None, CONV_WIDTH - 1, D_CONV), lambda bb, i: (bb, 0, 0)),
            pl.BlockSpec((None, CONV_WIDTH, D_CONV), lambda bb, i: (layer, 0, 0)),
            vec(), vec(), vec(),
        ],
        out_specs=[
            pl.BlockSpec((tm, D_CONV), lambda bb, i: (bb * nt + i, 0)),
            pl.BlockSpec((None, CONV_WIDTH - 1, D_CONV), lambda bb, i: (bb, 0, 0)),
        ],
        out_shape=[
            jax.ShapeDtypeStruct((n, D_CONV), BF16),
            jax.ShapeDtypeStruct((bsz, CONV_WIDTH - 1, D_CONV), F32),
        ],
        scratch_shapes=[pltpu.VMEM((SUBLANES, tm + CONV_PAD, D_CONV), F32)],
        compiler_params=_cparams(("parallel", "arbitrary")),
        name="conv_module",
    )(u, prev, conv_w, conv_b.reshape(-1, 1, D_CONV), g.reshape(-1, 1, D_CONV), b.reshape(-1, 1, D_CONV))


def _rel_bucket(rel):
    half = N_BUCKETS // 2
    max_exact = half // 2
    ret = jnp.where(rel > 0, half, 0)
    n = jnp.abs(rel)
    nf = jnp.maximum(n, 1).astype(F32)
    large = max_exact + (jnp.log(nf / max_exact) / math.log(MAX_DISTANCE / max_exact)
                         * (half - max_exact)).astype(jnp.int32)
    large = jnp.minimum(large, half - 1)
    return ret + jnp.where(n < max_exact, n, large)


def _toeplitz_bias(rel_bias, rel_of_diff, rows, cols):
    period = rows + cols
    x = np.arange(period)
    diff = np.where(x < cols, x, x - period)
    rel = jnp.asarray(rel_of_diff(diff), jnp.int32)
    wp = rel_bias[_rel_bucket(rel)].astype(F32).T
    flat = jnp.tile(wp, (1, rows))[:, :rows * (period - 1)]
    return flat.reshape(-1, rows, period - 1)[:, :, :cols]


def _lam_value(lam_ref, lam_init):
    lf = lam_ref[...]
    a = jnp.sum(lf[0:1, :] * lf[1:2, :], axis=1, keepdims=True)
    b = jnp.sum(lf[2:3, :] * lf[3:4, :], axis=1, keepdims=True)
    return jnp.exp(a) - jnp.exp(b) + lam_init


def _block_diag_q(q):
    lane = lax.broadcasted_iota(jnp.int32, q.shape, 1)
    zero = jnp.zeros_like(q)
    return jnp.concatenate([jnp.where(lane < DIFF_HD, q, zero), jnp.where(lane >= DIFF_HD, q, zero)], axis=0)


def _attn_prompt_body(t, tv, nq, lam_init, q_ref, k_ref, vt_ref, bias_ref, lam_ref, subg_ref, o_ref,
                      qbd_ref, pa_ref, pb_ref, r_ref, g_ref, acc_ref):
    i = pl.program_id(2)
    sub = t // tv
    rb = ATTN_ROW_BLOCK
    lanes = 2 * t
    q_t = q_ref[...].astype(F32).T.astype(BF16)
    head_row = lax.broadcasted_iota(jnp.int32, q_t.shape, 0)
    zero = jnp.zeros_like(q_t)
    qbd_ref[:, 0:t] = jnp.where(head_row < DIFF_HD, q_t, zero)
    qbd_ref[:, t:lanes] = jnp.where(head_row >= DIFF_HD, q_t, zero)

    def tile_scores(j, bias_idx):
        k = k_ref[pl.ds(pl.multiple_of(j * t, t), t), :]
        s = _dot(k, qbd_ref[...])
        if bias_idx is not None:
            bias = bias_ref[bias_idx]
            s = s + jnp.concatenate([bias, bias], axis=1)
        return s

    def column_max(s):
        return jnp.max(s.reshape(t // 8, 8, lanes), axis=0)

    def step(j, bias_idx, p_ref):
        s = tile_scores(j, bias_idx)
        g_ref[...] = jnp.maximum(g_ref[...], column_max(s))
        r = r_ref[...]
        for c in range(t // rb):
            p_ref[c * rb:(c + 1) * rb, :] = jnp.exp2((s[c * rb:(c + 1) * rb, :] - r).astype(BF16))
        pv = None
        for u in range(sub):
            part = _dot(vt_ref[j * sub + u], p_ref[u * tv:(u + 1) * tv, :])
            pv = part if pv is None else pv + part
        acc_ref[...] += pv

    def sweep():
        acc_ref[...] = jnp.zeros_like(acc_ref)
        g_ref[...] = jnp.full(g_ref.shape, MASKED, F32)

        @pl.when(i == 0)
        def _():
            step(i, 0, pa_ref)

        @pl.when(i >= 1)
        def _():
            step(i, 0, pa_ref)
            step(i - 1, 1, pb_ref)

        n_far = jnp.maximum(i - 1, 0)
        odd = n_far % 2

        @pl.when(odd == 1)
        def _():
            step(i - 2, None, pa_ref)

        def far_pair(c, carry):
            j = i - 2 - odd - 2 * c
            step(j, None, pa_ref)
            step(j - 1, None, pb_ref)
            return carry

        lax.fori_loop(0, n_far // 2, far_pair, 0)

    r_ref[...] = jnp.max(column_max(tile_scores(i, 0)), axis=0, keepdims=True)
    sweep()
    true_max = jnp.max(g_ref[...], axis=0, keepdims=True)

    @pl.when(jnp.max(true_max - r_ref[...]) > REF_SLACK)
    def _():
        r_ref[...] = true_max
        sweep()

    acc = acc_ref[...]
    o = acc[0:DIFF_VD, :] / acc[DIFF_VD:DIFF_VD + 1, :]
    lam = _lam_value(lam_ref, lam_init)
    a = o[:, :t] - lam * o[:, t:]
    a = a * lax.rsqrt(jnp.mean(a * a, axis=0, keepdims=True) + LN_EPS)
    o_ref[...] = (a.T * subg_ref[...] * (1.0 - lam_init)).astype(o_ref.dtype)


def _attn_prompt(q, kb, vt, rel_bias, diff_lam, sub_g, layer, lam_init, bsz, seq, t):
    tv = vt.shape[-1]
    assert t >= MAX_DISTANCE and t % CHUNK == 0 and seq % t == 0 and t % tv == 0
    n = q.shape[0]
    nq = seq // t
    far_bias = rel_bias[_rel_bucket(jnp.full((), -MAX_DISTANCE, jnp.int32))].astype(F32)[:, None, None]
    b0 = _toeplitz_bias(rel_bias, lambda d: -d, t, t) - far_bias
    b1 = _toeplitz_bias(rel_bias, lambda d: -d - t, t, t) - far_bias
    r = jnp.arange(t, dtype=jnp.int32)[:, None]
    c = jnp.arange(t, dtype=jnp.int32)[None, :]
    b0 = jnp.where((r // CHUNK) <= (c // CHUNK), b0 * LOG2E, MASKED)
    tiles = jnp.stack([b0, b1 * LOG2E], axis=1)

    return pl.pallas_call(
        functools.partial(_attn_prompt_body, t, tv, nq, lam_init),
        grid=(bsz, N_DIFF, nq),
        in_specs=[
            pl.BlockSpec((t, DIFF_VD), lambda b, h, i: (b * nq + i, h)),
            pl.BlockSpec((seq, DIFF_VD), lambda b, h, i: (b, h)),
            pl.BlockSpec((None, None, seq // tv, VT_ROWS, tv), lambda b, h, i: (b, h, 0, 0, 0)),
            pl.BlockSpec((None, 2, t, t), lambda b, h, i: (h, 0, 0, 0), pipeline_mode=pl.Buffered(1)),
            pl.BlockSpec((None, 4, DIFF_HD), lambda b, h, i: (layer, 0, 0)),
            pl.BlockSpec((None, 1, DIFF_VD), lambda b, h, i: (layer, 0, 0)),
        ],
        out_specs=pl.BlockSpec((t, DIFF_VD), lambda b, h, i: (b * nq + i, h)),
        out_shape=jax.ShapeDtypeStruct((n, D_ATT), BF16),
        scratch_shapes=[
            pltpu.VMEM((DIFF_VD, 2 * t), BF16),
            pltpu.VMEM((t, 2 * t), BF16),
            pltpu.VMEM((t, 2 * t), BF16),
            pltpu.VMEM((1, 2 * t), F32),
            pltpu.VMEM((8, 2 * t), F32),
            pltpu.VMEM((VT_ROWS, 2 * t), F32),
        ],
        compiler_params=_cparams(("parallel", "parallel", "arbitrary"), V7X_VMEM_LIMIT),
        name="attn_prompt",
    )(q, kb, vt, tiles, diff_lam, sub_g.reshape(-1, 1, DIFF_VD))


def _attn_sample_body(tq, lam_init, q_ref, kp_ref, kn_ref, vp_ref, vn_ref, bp_ref, bn_ref, lam_ref, subg_ref, o_ref):
    qbd = _block_diag_q(q_ref[...])
    s_p = _dot_nt(qbd, kp_ref[...].astype(BF16)) + bp_ref[...]
    s_n = _dot_nt(qbd, kn_ref[...]) + bn_ref[...]
    m = jnp.maximum(jnp.max(s_p, axis=1, keepdims=True), jnp.max(s_n, axis=1, keepdims=True))
    p_p = jnp.exp2(s_p - m)
    p_n = jnp.exp2(s_n - m)
    l = jnp.sum(p_p, axis=1, keepdims=True) + jnp.sum(p_n, axis=1, keepdims=True)
    o = (_dot(p_p.astype(BF16), vp_ref[...].astype(BF16)) + _dot(p_n.astype(BF16), vn_ref[...].astype(BF16))) / l
    lam = _lam_value(lam_ref, lam_init)
    a = o[:tq, :] - lam * o[tq:, :]
    a = a * lax.rsqrt(jnp.mean(a * a, axis=1, keepdims=True) + LN_EPS)
    o_ref[...] = (a * subg_ref[...] * (1.0 - lam_init)).astype(o_ref.dtype)


def _attn_sample(q, kb, v_stack, cache_k, cache_v, rel_bias, diff_lam, sub_g, layer, lam_init, bsz, tq):
    assert tq <= 128
    n = q.shape[0]
    past = cache_k.shape[2]
    bias = _toeplitz_bias(rel_bias, lambda d: d - past, tq, past + tq) * LOG2E
    qpos = past + jnp.arange(tq, dtype=jnp.int32)[:, None]
    kpos = jnp.arange(past + tq, dtype=jnp.int32)[None, :]
    bias = jnp.where((kpos // CHUNK) <= (qpos // CHUNK), bias, MASKED)
    bias = jnp.concatenate([bias, bias], axis=1)
    cache_k = cache_k.reshape(cache_k.shape[:3] + (D_ATT,))
    cache_v = cache_v.reshape(cache_v.shape[:3] + (D_ATT,))
    cache_spec = pl.BlockSpec((None, None, past, DIFF_VD), lambda b, h: (layer, b, 0, h))
    new_spec = pl.BlockSpec((tq, DIFF_VD), lambda b, h: (b, h))
    return pl.pallas_call(
        functools.partial(_attn_sample_body, tq, lam_init),
        grid=(bsz, N_DIFF),
        in_specs=[
            new_spec, cache_spec, new_spec, cache_spec,
            pl.BlockSpec((None, tq, DIFF_VD), lambda b, h: (layer, b, h)),
            pl.BlockSpec((None, 2 * tq, past), lambda b, h: (h, 0, 0)),
            pl.BlockSpec((None, 2 * tq, tq), lambda b, h: (h, 0, 0)),
            pl.BlockSpec((None, 4, DIFF_HD), lambda b, h: (layer, 0, 0)),
            pl.BlockSpec((None, 1, DIFF_VD), lambda b, h: (layer, 0, 0)),
        ],
        out_specs=new_spec,
        out_shape=jax.ShapeDtypeStruct((n, D_ATT), BF16),
        compiler_params=_cparams(("parallel", "parallel")),
        name="attn_sample",
    )(q, cache_k, kb, cache_v, v_stack, bias[:, :, :past], bias[:, :, past:], diff_lam, sub_g.reshape(-1, 1, DIFF_VD))


def _retention_body(nc, q_ref, k_ref, kw_ref, v_ref, gate_ref, s0_ref, decay_ref, wq_ref, gl_ref,
                    o_ref, snew_ref, s_ref):
    c = pl.program_id(1)

    @pl.when(c == 0)
    def _():
        s_ref[...] = s0_ref[...]

    for h in range(N_RET):
        cols = slice(h * RET_HD, (h + 1) * RET_HD)
        q = q_ref[:, cols]
        v = v_ref[:, cols]
        s_prev = s_ref[h]
        inner = _dot_nt(q, k_ref[:, cols]) * decay_ref[h]
        o = _dot(inner.astype(BF16), v) + _dot(q, s_prev.astype(BF16)) * wq_ref[:, cols]
        s_ref[h] = gl_ref[h] * s_prev + _dot_tn(kw_ref[:, cols], v)
        mu = jnp.mean(o, axis=1, keepdims=True)
        d = o - mu
        var = jnp.mean(d * d, axis=1, keepdims=True)
        o_ref[:, cols] = (gate_ref[:, cols] * (d * lax.rsqrt(var + LN_EPS))).astype(o_ref.dtype)

    @pl.when(c == nc - 1)
    def _():
        snew_ref[...] = s_ref[...]


def _retention_tables(lc):
    log_g = jnp.log1p(-jnp.exp2(-5.0 - jnp.arange(N_RET, dtype=F32)))
    idx = jnp.arange(lc, dtype=F32)
    rel = idx[:, None] - idx[None, :]
    decay = jnp.where(rel >= 0, jnp.exp(jnp.maximum(rel, 0.0)[None] * log_g[:, None, None]), 0.0)
    w_k = jnp.exp((lc - 1.0 - idx)[None, :] * log_g[:, None])
    w_q = jnp.exp((idx + 1.0)[None, :] * log_g[:, None])
    g_l = jnp.exp(lc * log_g)
    widen = lambda w: jnp.repeat(w.T, RET_HD, axis=1)
    gl_tab = jnp.broadcast_to(g_l[:, None, None], (N_RET, 1, RET_HD))
    return decay, widen(w_q), widen(w_k), gl_tab


def _retention(rq, rk, rkw, rv, gate, s0, decay, wq_tab, gl_tab, bsz, t, lc):
    n = rq.shape[0]
    nc = t // lc
    tile = pl.BlockSpec((lc, D_RET), lambda b, c: (b * nc + c, 0))
    state = pl.BlockSpec((None, N_RET, RET_HD, RET_HD), lambda b, c: (b, 0, 0, 0))
    return pl.pallas_call(
        functools.partial(_retention_body, nc),
        grid=(bsz, nc),
        in_specs=[
            tile, tile, tile, tile, tile, state,
            pl.BlockSpec((N_RET, lc, lc), lambda b, c: (0, 0, 0)),
            pl.BlockSpec((lc, D_RET), lambda b, c: (0, 0)),
            pl.BlockSpec((N_RET, 1, RET_HD), lambda b, c: (0, 0, 0)),
        ],
        out_specs=[tile, state],
        out_shape=[
            jax.ShapeDtypeStruct((n, D_RET), BF16),
            jax.ShapeDtypeStruct((bsz, N_RET, RET_HD, RET_HD), F32),
        ],
        scratch_shapes=[pltpu.VMEM((N_RET, RET_HD, RET_HD), F32)],
        compiler_params=_cparams(("parallel", "arbitrary")),
        name="retention",
    )(rq, rk, rkw, rv, gate, s0, decay, wq_tab, gl_tab)


def _out_proj_ln_body(alpha, x_ref, c_ref, a_ref, r_ref, w_ref, g_ref, b_ref, o_ref):
    e0 = D_CONV
    e1 = D_CONV + D_ATT
    mix = (_dot(c_ref[...], w_ref[0:e0, :]) + _dot(a_ref[...], w_ref[e0:e1, :])
           + _dot(r_ref[...], w_ref[e1:e1 + D_RET, :]))
    o_ref[...] = _layer_norm_rows(alpha * x_ref[...] + mix, g_ref[...], b_ref[...])


def _out_proj_ln(x, conv_out, attn_out, ret_out, w_out, g, b, layer, alpha, tm):
    n, d = x.shape
    d_mix = D_CONV + D_ATT + D_RET
    tile = lambda w: pl.BlockSpec((tm, w), lambda i: (i, 0))
    return pl.pallas_call(
        functools.partial(_out_proj_ln_body, alpha),
        grid=(n // tm,),
        in_specs=[
            tile(d), tile(D_CONV), tile(D_ATT), tile(D_RET),
            pl.BlockSpec((None, d_mix, d), lambda i: (layer, 0, 0), pipeline_mode=pl.Buffered(1)),
            pl.BlockSpec((1, d), lambda i: (0, 0)),
            pl.BlockSpec((1, d), lambda i: (0, 0)),
        ],
        out_specs=tile(d),
        out_shape=jax.ShapeDtypeStruct((n, d), F32),
        compiler_params=_cparams(("parallel",), V7X_VMEM_LIMIT),
        name="out_proj_ln",
    )(x, conv_out, attn_out, ret_out, w_out, g, b)


def _rope_tables(pos0, t):
    half = RET_HD // 2
    inv = 1.0 / (ROPE_BASE ** (jnp.arange(half, dtype=F32) / half))
    ang = (pos0 + jnp.arange(t, dtype=jnp.int32)).astype(F32)[:, None] * inv[None, :]
    cos = jnp.cos(ang)
    sin = jnp.sin(ang)
    return jnp.concatenate([cos, cos], axis=1), jnp.concatenate([-sin, sin], axis=1)


def _trunk(x, pos0, caches, params, tiles):
    (ln_g, ln_b, wg, wu, wd, w_in, w_out, conv_w, conv_b, conv_ln_g, conv_ln_b, diff_lam, diff_sub_g,
     rel_bias) = params
    bsz, t, d = x.shape
    depth = ln_g.shape[0]
    alpha = (2 * depth) ** 0.25
    tm_ffn, tf, tm_tok, tm_out, t_att = tiles
    prompt = caches is None
    lc = tm_tok
    cos, sin = _rope_tables(pos0, t)
    decay, wq_tab, wk_tab, gl_tab = _retention_tables(lc)
    x = x.reshape(bsz * t, d)
    conv_l, s_l = [], []
    kv_stacks = None
    for l in range(depth):
        lam_init = 0.8 - 0.6 * math.exp(-0.3 * l)
        if prompt:
            conv_prev = jnp.zeros((bsz, CONV_WIDTH - 1, D_CONV), F32)
            ret_prev = jnp.zeros((bsz, N_RET, RET_HD, RET_HD), F32)
        else:
            cache_conv, cache_k, cache_v, state_ret = caches
            conv_prev, ret_prev = cache_conv[l], state_ret[l]
        x = _ffn_ln(x, wg, wu, wd, ln_g[l, 0:1], ln_b[l, 0:1], l, 0, alpha, tm_ffn, tf)
        u, q, k_stack, kb, v_stack, vt, rq, rk, rkw, rv, gate = _in_proj(
            x, w_in, l, depth, kv_stacks, cos, sin, wk_tab, bsz, t, tm_tok)
        kv_stacks = (k_stack, v_stack)
        conv_out, conv_new = _conv_module(u, conv_prev, conv_w, conv_b, conv_ln_g, conv_ln_b, l, bsz, t, tm_tok)
        if prompt:
            attn_out = _attn_prompt(q, kb, vt, rel_bias, diff_lam, diff_sub_g, l, lam_init, bsz, t, t_att)
        else:
            attn_out = _attn_sample(q, kb, v_stack, cache_k, cache_v, rel_bias, diff_lam, diff_sub_g, l, lam_init, bsz,
                                    t)
        ret_out, s_new = _retention(rq, rk, rkw, rv, gate, ret_prev, decay, wq_tab, gl_tab, bsz, t, lc)
        x = _out_proj_ln(x, conv_out, attn_out, ret_out, w_out, ln_g[l, 1:2], ln_b[l, 1:2], l, alpha, tm_out)
        x = _ffn_ln(x, wg, wu, wd, ln_g[l, 2:3], ln_b[l, 2:3], l, 1, alpha, tm_ffn, tf)
        conv_l.append(conv_new)
        s_l.append(s_new)
    k_new, v_new = (a.reshape(depth, bsz, t, N_DIFF, DIFF_VD) for a in kv_stacks)
    return x.reshape(bsz, t, d), jnp.stack(conv_l), k_new, v_new, jnp.stack(s_l)


def _pick(n, cap):
    if n <= cap:
        return n
    best = None
    for c in range(128, cap + 1, 128):
        if n % c == 0:
            best = c
    assert best is not None
    return best


def kernel(x_prompt, x_sample, cache_conv, cache_k, cache_v, state_ret, ln_g, ln_b, ffn_w_gate, ffn_w_up,
           ffn_w_down, w_in, w_out, conv_w, conv_b, conv_ln_g, conv_ln_b, diff_lam, diff_sub_g, rel_bias):
    tf = _pick(ffn_w_gate.shape[-1], 512)
    params = (ln_g, ln_b, ffn_w_gate.astype(BF16), ffn_w_up.astype(BF16), ffn_w_down.astype(BF16),
              w_in.astype(BF16), w_out.astype(BF16), conv_w, conv_b, conv_ln_g, conv_ln_b, diff_lam, diff_sub_g,
              rel_bias)
    bp, tp, _ = x_prompt.shape
    bs, ts, _ = x_sample.shape
    tok_p = _pick(tp, 256)
    y_p, conv_p, k_p, v_p, ret_p = _trunk(
        x_prompt, 0, None, params, (_pick(bp * tp, 512), tf, tok_p, _pick(bp * tp, 512), _pick(tp, 512)))
    y_s, conv_s, k_s, v_s, ret_s = _trunk(
        x_sample, cache_k.shape[2], (cache_conv, cache_k, cache_v, state_ret), params,
        (_pick(bs * ts, 512), tf, ts, _pick(bs * ts, 512), None))
    return (y_p, y_s, conv_p, k_p, v_p, ret_p, conv_s, k_s, v_s, ret_s)
```

```python
import functools
import math

import jax
import jax.numpy as jnp
import numpy as np
from jax import lax
from jax.experimental import pallas as pl
from jax.experimental.pallas import tpu as pltpu

CHUNK = 64
CONV_WIDTH = 31
D_CONV = 512
N_DIFF = 8
DIFF_HD = 64
DIFF_VD = 2 * DIFF_HD
N_RET = 4
RET_HD = 128
N_BUCKETS = 32
MAX_DISTANCE = 128
LN_EPS = 1e-5
ROPE_BASE = 10000.0

D_ATT = N_DIFF * DIFF_VD
VT_ROWS = DIFF_VD + 16
D_RET = N_RET * RET_HD
C_CA, C_CG = 0, D_CONV
C_Q = 2 * D_CONV
C_K = C_Q + D_ATT
C_V = C_K + D_ATT
C_RQ = C_V + D_ATT
C_RK = C_RQ + D_RET
C_RV = C_RK + D_RET
C_RG = C_RV + D_RET
D_IN = C_RG + D_RET

LOG2E = math.log2(math.e)
REF_SLACK = 8.0
MASKED = -1e30
ATTN_ROW_BLOCK = 32
SUBLANES = 8
CONV_PAD = 32
V7X_VMEM_LIMIT = 56 * 1024 * 1024

F32 = jnp.float32
BF16 = jnp.bfloat16


def _cparams(sem, vmem=None):
    return pltpu.CompilerParams(dimension_semantics=sem, vmem_limit_bytes=vmem)


def _sigmoid(x):
    return 1.0 / (1.0 + jnp.exp(-x))


def _layer_norm_rows(y, g, b):
    mu = jnp.mean(y, axis=-1, keepdims=True)
    d = y - mu
    var = jnp.mean(d * d, axis=-1, keepdims=True)
    return d * lax.rsqrt(var + LN_EPS) * g + b


def _dot(a, b):
    return jnp.dot(a, b, preferred_element_type=F32)


def _dot_nt(a, b):
    return lax.dot_general(a, b, (((1,), (1,)), ((), ())), preferred_element_type=F32)


def _dot_tn(a, b):
    return lax.dot_general(a, b, (((0,), (0,)), ((), ())), preferred_element_type=F32)


def _ffn_ln_body(alpha, nf, x_ref, wg_ref, wu_ref, wd_ref, g_ref, b_ref, o_ref, xb_ref):
    f = pl.program_id(1)

    @pl.when(f == 0)
    def _():
        xb_ref[...] = x_ref[...].astype(BF16)
        o_ref[...] = jnp.zeros_like(o_ref)

    xb = xb_ref[...]
    hg = _dot(xb, wg_ref[...])
    hu = _dot(xb, wu_ref[...])
    act = (hg * _sigmoid(hg) * hu).astype(BF16)
    o_ref[...] += _dot(act, wd_ref[...])

    @pl.when(f == nf - 1)
    def _():
        y = alpha * x_ref[...] + 0.5 * o_ref[...]
        o_ref[...] = _layer_norm_rows(y, g_ref[...], b_ref[...])


def _ffn_ln(x, wg, wu, wd, g, b, layer, slot, alpha, tm, tf):
    n, d = x.shape
    nf = wg.shape[-1] // tf
    return pl.pallas_call(
        functools.partial(_ffn_ln_body, alpha, nf),
        grid=(n // tm, nf),
        in_specs=[
            pl.BlockSpec((tm, d), lambda i, f: (i, 0)),
            pl.BlockSpec((None, None, d, tf), lambda i, f: (layer, slot, 0, f)),
            pl.BlockSpec((None, None, d, tf), lambda i, f: (layer, slot, 0, f)),
            pl.BlockSpec((None, None, tf, d), lambda i, f: (layer, slot, f, 0)),
            pl.BlockSpec((1, d), lambda i, f: (0, 0)),
            pl.BlockSpec((1, d), lambda i, f: (0, 0)),
        ],
        out_specs=pl.BlockSpec((tm, d), lambda i, f: (i, 0)),
        out_shape=jax.ShapeDtypeStruct((n, d), F32),
        scratch_shapes=[pltpu.VMEM((tm, d), BF16)],
        compiler_params=_cparams(("parallel", "arbitrary"), V7X_VMEM_LIMIT),
        name="ffn_ln",
    )(x, wg, wu, wd, g, b)


def _rope_heads(x, cos, sin):
    outs = []
    for h in range(N_RET):
        xh = x[:, h * RET_HD:(h + 1) * RET_HD]
        outs.append(xh * cos + pltpu.roll(xh, RET_HD // 2, 1) * sin)
    return jnp.concatenate(outs, axis=1)


def _in_proj_body(tm, n_aliased, h_ref, w_ref, cos_ref, sin_ref, wk_ref, *refs):
    u_ref, q_ref, kf_ref, kb_ref, vf_ref, vt_ref, rq_ref, rk_ref, rkw_ref, rv_ref, rg_ref = refs[n_aliased:]
    hb = h_ref[...].astype(BF16)

    def mm(lo, hi):
        return _dot(hb, w_ref[:, lo:hi])

    u_ref[...] = mm(C_CA, C_CG) * _sigmoid(mm(C_CG, C_Q))
    q_ref[...] = (mm(C_Q, C_K) * (DIFF_HD ** -0.5 * LOG2E)).astype(BF16)
    k = mm(C_K, C_V)
    kf_ref[...] = k
    kb_ref[...] = k.astype(BF16)
    v = mm(C_V, C_RQ)
    vf_ref[...] = v
    vt_ref[:, 0:DIFF_VD, :] = v.T.reshape(N_DIFF, DIFF_VD, tm).astype(BF16)
    pad_row = lax.broadcasted_iota(jnp.int32, (N_DIFF, VT_ROWS - DIFF_VD, tm), 1)
    vt_ref[:, DIFF_VD:VT_ROWS, :] = jnp.where(pad_row == 0, 1.0, 0.0).astype(BF16)
    cos = cos_ref[...]
    sin = sin_ref[...]
    rq_ref[...] = _rope_heads(mm(C_RQ, C_RK), cos, sin).astype(BF16)
    rk = _rope_heads(mm(C_RK, C_RV), cos, sin) * (RET_HD ** -0.5)
    rk_ref[...] = rk.astype(BF16)
    rkw_ref[...] = (rk * wk_ref[...]).astype(BF16)
    rv_ref[...] = mm(C_RV, C_RG).astype(BF16)
    rg = mm(C_RG, D_IN)
    rg_ref[...] = rg * _sigmoid(rg)


def _in_proj(h, w_in, layer, depth, kv_stacks, cos, sin, wk_tab, bsz, t, tm):
    n, d = h.shape
    nt = t // tm
    row = lambda b, i: (b * nt + i, 0)
    STACK, VT = "stack", "vt"
    outs = [
        ((n, D_CONV), F32),
        ((n, D_ATT), BF16),
        STACK,
        ((n, D_ATT), BF16),
        STACK,
        VT,
        ((n, D_RET), BF16),
        ((n, D_RET), BF16),
        ((n, D_RET), BF16),
        ((n, D_RET), BF16),
        ((n, D_RET), F32),
    ]
    out_shape, out_specs = [], []
    for o in outs:
        if o is VT:
            out_shape.append(jax.ShapeDtypeStruct((bsz, N_DIFF, nt, VT_ROWS, tm), BF16))
            out_specs.append(pl.BlockSpec((None, N_DIFF, None, VT_ROWS, tm), lambda b, i: (b, 0, i, 0, 0)))
        elif o is STACK:
            out_shape.append(jax.ShapeDtypeStruct((depth, n, D_ATT), F32))
            out_specs.append(pl.BlockSpec((None, tm, D_ATT), lambda b, i: (layer, b * nt + i, 0)))
        else:
            out_shape.append(jax.ShapeDtypeStruct(*o))
            out_specs.append(pl.BlockSpec((tm, o[0][1]), row))
    in_specs = [
        pl.BlockSpec((tm, d), row),
        pl.BlockSpec((None, d, D_IN), lambda b, i: (layer, 0, 0), pipeline_mode=pl.Buffered(1)),
        pl.BlockSpec((tm, RET_HD), lambda b, i: (i, 0)),
        pl.BlockSpec((tm, RET_HD), lambda b, i: (i, 0)),
        pl.BlockSpec((tm, D_RET), lambda b, i: (0, 0)),
    ]
    operands = [h, w_in, cos, sin, wk_tab]
    aliases = {}
    if kv_stacks is not None:
        for stack, out_idx in zip(kv_stacks, [k for k, o in enumerate(outs) if o is STACK]):
            aliases[len(operands)] = out_idx
            in_specs.append(pl.BlockSpec(memory_space=pl.ANY))
            operands.append(stack)
    return pl.pallas_call(
        functools.partial(_in_proj_body, tm, len(aliases)),
        grid=(bsz, nt),
        in_specs=in_specs,
        out_specs=out_specs,
        out_shape=out_shape,
        input_output_aliases=aliases,
        compiler_params=_cparams(("parallel", "parallel"), V7X_VMEM_LIMIT),
        name="in_proj",
    )(*operands)


def _conv_body(tm, nt, u_ref, prev_ref, w_ref, cb_ref, g_ref, b_ref, o_ref, new_ref, ext_ref):
    i = pl.program_id(1)
    hist = CONV_WIDTH - 1

    @pl.when(i == 0)
    def _():
        ext_ref[0, 0:CONV_PAD - hist, :] = jnp.zeros((CONV_PAD - hist, D_CONV), F32)
        ext_ref[0, CONV_PAD - hist:CONV_PAD, :] = prev_ref[...]

    @pl.when(i > 0)
    def _():
        ext_ref[0, 0:CONV_PAD, :] = ext_ref[0, tm:tm + CONV_PAD, :]

    ext_ref[0, CONV_PAD:CONV_PAD + tm, :] = u_ref[...]
    shifted_rows = tm + CONV_PAD - SUBLANES
    for s in range(1, SUBLANES):
        ext_ref[s, 0:shifted_rows, :] = ext_ref[0, s:s + shifted_rows, :]

    rb = 32
    cb = cb_ref[...]
    g = g_ref[...]
    b = b_ref[...]
    for r0 in range(0, tm, rb):
        acc = jnp.zeros((rb, D_CONV), F32)
        for w in range(CONV_WIDTH):
            shift = (CONV_PAD - hist + w) % SUBLANES
            start = CONV_PAD - hist + w - shift + r0
            acc = acc + ext_ref[shift, start:start + rb, :] * w_ref[w:w + 1, :]
        y = _layer_norm_rows(acc + cb, g, b)
        o_ref[r0:r0 + rb, :] = (y * _sigmoid(y)).astype(o_ref.dtype)

    @pl.when(i == nt - 1)
    def _():
        new_ref[...] = ext_ref[0, CONV_PAD + tm - hist:CONV_PAD + tm, :]


def _conv_module(u, prev, conv_w, conv_b, g, b, layer, bsz, t, tm):
    n = u.shape[0]
    nt = t // tm
    assert t >= CONV_PAD and tm >= CONV_PAD and tm % 32 == 0
    vec = lambda: pl.BlockSpec((None, 1, D_CONV), lambda bb, i: (layer, 0, 0))
    return pl.pallas_call(
        functools.partial(_conv_body, tm, nt),
        grid=(bsz, nt),
        in_specs=[
            pl.BlockSpec((tm, D_CONV), lambda bb, i: (bb * nt + i, 0)),
            pl.BlockSpec((None, CONV_WIDTH - 1, D_CONV), lambda bb, i: (bb, 0, 0)),
            pl.BlockSpec((None, CONV_WIDTH, D_CONV), lambda bb, i: (layer, 0, 0)),
            vec(), vec(), vec(),
        ],
        out_specs=[
            pl.BlockSpec((tm, D_CONV), lambda bb, i: (bb * nt + i, 0)),
            pl.BlockSpec((None, CONV_WIDTH - 1, D_CONV), lambda bb, i: (bb, 0, 0)),
        ],
        out_shape=[
            jax.ShapeDtypeStruct((n, D_CONV), BF16),
            jax.ShapeDtypeStruct((bsz, CONV_WIDTH - 1, D_CONV), F32),
        ],
        scratch_shapes=[pltpu.VMEM((SUBLANES, tm + CONV_PAD, D_CONV), F32)],
        compiler_params=_cparams(("parallel", "arbitrary")),
        name="conv_module",
    )(u, prev, conv_w, conv_b.reshape(-1, 1, D_CONV), g.reshape(-1, 1, D_CONV), b.reshape(-1, 1, D_CONV))


def _rel_bucket(rel):
    half = N_BUCKETS // 2
    max_exact = half // 2
    ret = jnp.where(rel > 0, half, 0)
    n = jnp.abs(rel)
    nf = jnp.maximum(n, 1).astype(F32)
    large = max_exact + (jnp.log(nf / max_exact) / math.log(MAX_DISTANCE / max_exact)
                         * (half - max_exact)).astype(jnp.int32)
    large = jnp.minimum(large, half - 1)
    return ret + jnp.where(n < max_exact, n, large)


def _toeplitz_bias(rel_bias, rel_of_diff, rows, cols):
    period = rows + cols
    x = np.arange(period)
    diff = np.where(x < cols, x, x - period)
    rel = jnp.asarray(rel_of_diff(diff), jnp.int32)
    wp = rel_bias[_rel_bucket(rel)].astype(F32).T
    flat = jnp.tile(wp, (1, rows))[:, :rows * (period - 1)]
    return flat.reshape(-1, rows, period - 1)[:, :, :cols]


def _lam_value(lam_ref, lam_init):
    lf = lam_ref[...]
    a = jnp.sum(lf[0:1, :] * lf[1:2, :], axis=1, keepdims=True)
    b = jnp.sum(lf[2:3, :] * lf[3:4, :], axis=1, keepdims=True)
    return jnp.exp(a) - jnp.exp(b) + lam_init


def _block_diag_q(q):
    lane = lax.broadcasted_iota(jnp.int32, q.shape, 1)
    zero = jnp.zeros_like(q)
    return jnp.concatenate([jnp.where(lane < DIFF_HD, q, zero), jnp.where(lane >= DIFF_HD, q, zero)], axis=0)


def _attn_prompt_body(t, tv, nq, lam_init, q_ref, k_ref, vt_ref, bias_ref, lam_ref, subg_ref, o_ref,
                      qbd_ref, pa_ref, pb_ref, r_ref, g_ref, acc_ref):
    i = pl.program_id(2)
    sub = t // tv
    rb = ATTN_ROW_BLOCK
    lanes = 2 * t
    q_t = q_ref[...].astype(F32).T.astype(BF16)
    head_row = lax.broadcasted_iota(jnp.int32, q_t.shape, 0)
    zero = jnp.zeros_like(q_t)
    qbd_ref[:, 0:t] = jnp.where(head_row < DIFF_HD, q_t, zero)
    qbd_ref[:, t:lanes] = jnp.where(head_row >= DIFF_HD, q_t, zero)

    def tile_scores(j, bias_idx):
        k = k_ref[pl.ds(pl.multiple_of(j * t, t), t), :]
        s = _dot(k, qbd_ref[...])
        if bias_idx is not None:
            bias = bias_ref[bias_idx]
            s = s + jnp.concatenate([bias, bias], axis=1)
        return s

    def column_max(s):
        return jnp.max(s.reshape(t // 8, 8, lanes), axis=0)

    def step(j, bias_idx, p_ref, s=None):
        if s is None:
            s = tile_scores(j, bias_idx)
        g_ref[...] = jnp.maximum(g_ref[...], column_max(s))
        r = r_ref[...]
        for c in range(t // rb):
            p_ref[c * rb:(c + 1) * rb, :] = jnp.exp2((s[c * rb:(c + 1) * rb, :] - r).astype(BF16))
        pv = None
        for u in range(sub):
            part = _dot(vt_ref[j * sub + u], p_ref[u * tv:(u + 1) * tv, :])
            pv = part if pv is None else pv + part
        acc_ref[...] += pv

    def sweep(reference_from_diagonal):
        acc_ref[...] = jnp.zeros_like(acc_ref)
        g_ref[...] = jnp.full(g_ref.shape, MASKED, F32)

        def diagonal_scores():
            s = tile_scores(i, 0)
            if reference_from_diagonal:
                r_ref[...] = jnp.max(column_max(s), axis=0, keepdims=True)
            return s

        @pl.when(i == 0)
        def _():
            step(i, 0, pa_ref, diagonal_scores())

        @pl.when(i >= 1)
        def _():
            s_b = tile_scores(i - 1, 1)
            s_a = diagonal_scores()
            step(i, 0, pa_ref, s_a)
            step(i - 1, 1, pb_ref, s_b)

        n_far = jnp.maximum(i - 1, 0)
        rem = n_far % 4

        @pl.when(rem % 2 == 1)
        def _():
            step(i - 2, None, pa_ref)

        @pl.when(rem >= 2)
        def _():
            j = i - 2 - rem % 2
            s_b = tile_scores(j - 1, None)
            step(j, None, pa_ref)
            step(j - 1, None, pb_ref, s_b)

        def far_quad(c, carry):
            j = i - 2 - rem - 4 * c
            s_a = tile_scores(j, None)
            s_b = tile_scores(j - 1, None)
            s_c = tile_scores(j - 2, None)
            step(j, None, pa_ref, s_a)
            s_d = tile_scores(j - 3, None)
            step(j - 1, None, pb_ref, s_b)
            step(j - 2, None, pa_ref, s_c)
            step(j - 3, None, pb_ref, s_d)
            return carry

        lax.fori_loop(0, n_far // 4, far_quad, 0)

    sweep(True)
    true_max = jnp.max(g_ref[...], axis=0, keepdims=True)

    @pl.when(jnp.max(true_max - r_ref[...]) > REF_SLACK)
    def _():
        r_ref[...] = true_max
        sweep(False)

    acc = acc_ref[...]
    o = acc[0:DIFF_VD, :] / acc[DIFF_VD:DIFF_VD + 1, :]
    lam = _lam_value(lam_ref, lam_init)
    a = o[:, :t] - lam * o[:, t:]
    a = a * lax.rsqrt(jnp.mean(a * a, axis=0, keepdims=True) + LN_EPS)
    o_ref[...] = (a.T * subg_ref[...] * (1.0 - lam_init)).astype(o_ref.dtype)


def _attn_prompt(q, kb, vt, rel_bias, diff_lam, sub_g, layer, lam_init, bsz, seq, t):
    tv = vt.shape[-1]
    assert t >= MAX_DISTANCE and t % CHUNK == 0 and seq % t == 0 and t % tv == 0
    n = q.shape[0]
    nq = seq // t
    far_bias = rel_bias[_rel_bucket(jnp.full((), -MAX_DISTANCE, jnp.int32))].astype(F32)[:, None, None]
    b0 = _toeplitz_bias(rel_bias, lambda d: -d, t, t) - far_bias
    b1 = _toeplitz_bias(rel_bias, lambda d: -d - t, t, t) - far_bias
    r = jnp.arange(t, dtype=jnp.int32)[:, None]
    c = jnp.arange(t, dtype=jnp.int32)[None, :]
    b0 = jnp.where((r // CHUNK) <= (c // CHUNK), b0 * LOG2E, MASKED)
    tiles = jnp.stack([b0, b1 * LOG2E], axis=1)

    return pl.pallas_call(
        functools.partial(_attn_prompt_body, t, tv, nq, lam_init),
        grid=(bsz, N_DIFF, nq),
        in_specs=[
            pl.BlockSpec((t, DIFF_VD), lambda b, h, i: (b * nq + i, h)),
            pl.BlockSpec((seq, DIFF_VD), lambda b, h, i: (b, h)),
            pl.BlockSpec((None, None, seq // tv, VT_ROWS, tv), lambda b, h, i: (b, h, 0, 0, 0)),
            pl.BlockSpec((None, 2, t, t), lambda b, h, i: (h, 0, 0, 0), pipeline_mode=pl.Buffered(1)),
            pl.BlockSpec((None, 4, DIFF_HD), lambda b, h, i: (layer, 0, 0)),
            pl.BlockSpec((None, 1, DIFF_VD), lambda b, h, i: (layer, 0, 0)),
        ],
        out_specs=pl.BlockSpec((t, DIFF_VD), lambda b, h, i: (b * nq + i, h)),
        out_shape=jax.ShapeDtypeStruct((n, D_ATT), BF16),
        scratch_shapes=[
            pltpu.VMEM((DIFF_VD, 2 * t), BF16),
            pltpu.VMEM((t, 2 * t), BF16),
            pltpu.VMEM((t, 2 * t), BF16),
            pltpu.VMEM((1, 2 * t), F32),
            pltpu.VMEM((8, 2 * t), F32),
            pltpu.VMEM((VT_ROWS, 2 * t), F32),
        ],
        compiler_params=_cparams(("parallel", "parallel", "arbitrary"), V7X_VMEM_LIMIT),
        name="attn_prompt",
    )(q, kb, vt, tiles, diff_lam, sub_g.reshape(-1, 1, DIFF_VD))


def _attn_sample_body(tq, lam_init, q_ref, kp_ref, kn_ref, vp_ref, vn_ref, bp_ref, bn_ref, lam_ref, subg_ref, o_ref):
    qbd = _block_diag_q(q_ref[...])
    s_p = _dot_nt(qbd, kp_ref[...].astype(BF16)) + bp_ref[...]
    s_n = _dot_nt(qbd, kn_ref[...]) + bn_ref[...]
    m = jnp.maximum(jnp.max(s_p, axis=1, keepdims=True), jnp.max(s_n, axis=1, keepdims=True))
    p_p = jnp.exp2(s_p - m)
    p_n = jnp.exp2(s_n - m)
    l = jnp.sum(p_p, axis=1, keepdims=True) + jnp.sum(p_n, axis=1, keepdims=True)
    o = (_dot(p_p.astype(BF16), vp_ref[...].astype(BF16)) + _dot(p_n.astype(BF16), vn_ref[...].astype(BF16))) / l
    lam = _lam_value(lam_ref, lam_init)
    a = o[:tq, :] - lam * o[tq:, :]
    a = a * lax.rsqrt(jnp.mean(a * a, axis=1, keepdims=True) + LN_EPS)
    o_ref[...] = (a * subg_ref[...] * (1.0 - lam_init)).astype(o_ref.dtype)


def _attn_sample(q, kb, v_stack, cache_k, cache_v, rel_bias, diff_lam, sub_g, layer, lam_init, bsz, tq):
    assert tq <= 128
    n = q.shape[0]
    past = cache_k.shape[2]
    bias = _toeplitz_bias(rel_bias, lambda d: d - past, tq, past + tq) * LOG2E
    qpos = past + jnp.arange(tq, dtype=jnp.int32)[:, None]
    kpos = jnp.arange(past + tq, dtype=jnp.int32)[None, :]
    bias = jnp.where((kpos // CHUNK) <= (qpos // CHUNK), bias, MASKED)
    bias = jnp.concatenate([bias, bias], axis=1)
    cache_k = cache_k.reshape(cache_k.shape[:3] + (D_ATT,))
    cache_v = cache_v.reshape(cache_v.shape[:3] + (D_ATT,))
    cache_spec = pl.BlockSpec((None, None, past, DIFF_VD), lambda b, h: (layer, b, 0, h))
    new_spec = pl.BlockSpec((tq, DIFF_VD), lambda b, h: (b, h))
    return pl.pallas_call(
        functools.partial(_attn_sample_body, tq, lam_init),
        grid=(bsz, N_DIFF),
        in_specs=[
            new_spec, cache_spec, new_spec, cache_spec,
            pl.BlockSpec((None, tq, DIFF_VD), lambda b, h: (layer, b, h)),
            pl.BlockSpec((None, 2 * tq, past), lambda b, h: (h, 0, 0)),
            pl.BlockSpec((None, 2 * tq, tq), lambda b, h: (h, 0, 0)),
            pl.BlockSpec((None, 4, DIFF_HD), lambda b, h: (layer, 0, 0)),
            pl.BlockSpec((None, 1, DIFF_VD), lambda b, h: (layer, 0, 0)),
        ],
        out_specs=new_spec,
        out_shape=jax.ShapeDtypeStruct((n, D_ATT), BF16),
        compiler_params=_cparams(("parallel", "parallel")),
        name="attn_sample",
    )(q, cache_k, kb, cache_v, v_stack, bias[:, :, :past], bias[:, :, past:], diff_lam, sub_g.reshape(-1, 1, DIFF_VD))


def _retention_body(nc, q_ref, k_ref, kw_ref, v_ref, gate_ref, s0_ref, decay_ref, wq_ref, gl_ref,
                    o_ref, snew_ref, s_ref):
    c = pl.program_id(1)

    @pl.when(c == 0)
    def _():
        s_ref[...] = s0_ref[...]

    for h in range(N_RET):
        cols = slice(h * RET_HD, (h + 1) * RET_HD)
        q = q_ref[:, cols]
        v = v_ref[:, cols]
        s_prev = s_ref[h]
        inner = _dot_nt(q, k_ref[:, cols]) * decay_ref[h]
        o = _dot(inner.astype(BF16), v) + _dot(q, s_prev.astype(BF16)) * wq_ref[:, cols]
        s_ref[h] = gl_ref[h] * s_prev + _dot_tn(kw_ref[:, cols], v)
        mu = jnp.mean(o, axis=1, keepdims=True)
        d = o - mu
        var = jnp.mean(d * d, axis=1, keepdims=True)
        o_ref[:, cols] = (gate_ref[:, cols] * (d * lax.rsqrt(var + LN_EPS))).astype(o_ref.dtype)

    @pl.when(c == nc - 1)
    def _():
        snew_ref[...] = s_ref[...]


def _retention_tables(lc):
    log_g = jnp.log1p(-jnp.exp2(-5.0 - jnp.arange(N_RET, dtype=F32)))
    idx = jnp.arange(lc, dtype=F32)
    rel = idx[:, None] - idx[None, :]
    decay = jnp.where(rel >= 0, jnp.exp(jnp.maximum(rel, 0.0)[None] * log_g[:, None, None]), 0.0)
    w_k = jnp.exp((lc - 1.0 - idx)[None, :] * log_g[:, None])
    w_q = jnp.exp((idx + 1.0)[None, :] * log_g[:, None])
    g_l = jnp.exp(lc * log_g)
    widen = lambda w: jnp.repeat(w.T, RET_HD, axis=1)
    gl_tab = jnp.broadcast_to(g_l[:, None, None], (N_RET, 1, RET_HD))
    return decay, widen(w_q), widen(w_k), gl_tab


def _retention(rq, rk, rkw, rv, gate, s0, decay, wq_tab, gl_tab, bsz, t, lc):
    n = rq.shape[0]
    nc = t // lc
    tile = pl.BlockSpec((lc, D_RET), lambda b, c: (b * nc + c, 0))
    state = pl.BlockSpec((None, N_RET, RET_HD, RET_HD), lambda b, c: (b, 0, 0, 0))
    return pl.pallas_call(
        functools.partial(_retention_body, nc),
        grid=(bsz, nc),
        in_specs=[
            tile, tile, tile, tile, tile, state,
            pl.BlockSpec((N_RET, lc, lc), lambda b, c: (0, 0, 0)),
            pl.BlockSpec((lc, D_RET), lambda b, c: (0, 0)),
            pl.BlockSpec((N_RET, 1, RET_HD), lambda b, c: (0, 0, 0)),
        ],
        out_specs=[tile, state],
        out_shape=[
            jax.ShapeDtypeStruct((n, D_RET), BF16),
            jax.ShapeDtypeStruct((bsz, N_RET, RET_HD, RET_HD), F32),
        ],
        scratch_shapes=[pltpu.VMEM((N_RET, RET_HD, RET_HD), F32)],
        compiler_params=_cparams(("parallel", "arbitrary")),
        name="retention",
    )(rq, rk, rkw, rv, gate, s0, decay, wq_tab, gl_tab)


def _out_proj_ln_body(alpha, x_ref, c_ref, a_ref, r_ref, w_ref, g_ref, b_ref, o_ref):
    e0 = D_CONV
    e1 = D_CONV + D_ATT
    mix = (_dot(c_ref[...], w_ref[0:e0, :]) + _dot(a_ref[...], w_ref[e0:e1, :])
           + _dot(r_ref[...], w_ref[e1:e1 + D_RET, :]))
    o_ref[...] = _layer_norm_rows(alpha * x_ref[...] + mix, g_ref[...], b_ref[...])


def _out_proj_ln(x, conv_out, attn_out, ret_out, w_out, g, b, layer, alpha, tm):
    n, d = x.shape
    d_mix = D_CONV + D_ATT + D_RET
    tile = lambda w: pl.BlockSpec((tm, w), lambda i: (i, 0))
    return pl.pallas_call(
        functools.partial(_out_proj_ln_body, alpha),
        grid=(n // tm,),
        in_specs=[
            tile(d), tile(D_CONV), tile(D_ATT), tile(D_RET),
            pl.BlockSpec((None, d_mix, d), lambda i: (layer, 0, 0), pipeline_mode=pl.Buffered(1)),
            pl.BlockSpec((1, d), lambda i: (0, 0)),
            pl.BlockSpec((1, d), lambda i: (0, 0)),
        ],
        out_specs=tile(d),
        out_shape=jax.ShapeDtypeStruct((n, d), F32),
        compiler_params=_cparams(("parallel",), V7X_VMEM_LIMIT),
        name="out_proj_ln",
    )(x, conv_out, attn_out, ret_out, w_out, g, b)


def _rope_tables(pos0, t):
    half = RET_HD // 2
    inv = 1.0 / (ROPE_BASE ** (jnp.arange(half, dtype=F32) / half))
    ang = (pos0 + jnp.arange(t, dtype=jnp.int32)).astype(F32)[:, None] * inv[None, :]
    cos = jnp.cos(ang)
    sin = jnp.sin(ang)
    return jnp.concatenate([cos, cos], axis=1), jnp.concatenate([-sin, sin], axis=1)


def _trunk(x, pos0, caches, params, tiles):
    (ln_g, ln_b, wg, wu, wd, w_in, w_out, conv_w, conv_b, conv_ln_g, conv_ln_b, diff_lam, diff_sub_g,
     rel_bias) = params
    bsz, t, d = x.shape
    depth = ln_g.shape[0]
    alpha = (2 * depth) ** 0.25
    tm_ffn, tf, tm_tok, tm_out, t_att = tiles
    prompt = caches is None
    lc = tm_tok
    cos, sin = _rope_tables(pos0, t)
    decay, wq_tab, wk_tab, gl_tab = _retention_tables(lc)
    x = x.reshape(bsz * t, d)
    conv_l, s_l = [], []
    kv_stacks = None
    for l in range(depth):
        lam_init = 0.8 - 0.6 * math.exp(-0.3 * l)
        if prompt:
            conv_prev = jnp.zeros((bsz, CONV_WIDTH - 1, D_CONV), F32)
            ret_prev = jnp.zeros((bsz, N_RET, RET_HD, RET_HD), F32)
        else:
            cache_conv, cache_k, cache_v, state_ret = caches
            conv_prev, ret_prev = cache_conv[l], state_ret[l]
        x = _ffn_ln(x, wg, wu, wd, ln_g[l, 0:1], ln_b[l, 0:1], l, 0, alpha, tm_ffn, tf)
        u, q, k_stack, kb, v_stack, vt, rq, rk, rkw, rv, gate = _in_proj(
            x, w_in, l, depth, kv_stacks, cos, sin, wk_tab, bsz, t, tm_tok)
        kv_stacks = (k_stack, v_stack)
        conv_out, conv_new = _conv_module(u, conv_prev, conv_w, conv_b, conv_ln_g, conv_ln_b, l, bsz, t, tm_tok)
        if prompt:
            attn_out = _attn_prompt(q, kb, vt, rel_bias, diff_lam, diff_sub_g, l, lam_init, bsz, t, t_att)
        else:
            attn_out = _attn_sample(q, kb, v_stack, cache_k, cache_v, rel_bias, diff_lam, diff_sub_g, l, lam_init, bsz,
                                    t)
        ret_out, s_new = _retention(rq, rk, rkw, rv, gate, ret_prev, decay, wq_tab, gl_tab, bsz, t, lc)
        x = _out_proj_ln(x, conv_out, attn_out, ret_out, w_out, ln_g[l, 1:2], ln_b[l, 1:2], l, alpha, tm_out)
        x = _ffn_ln(x, wg, wu, wd, ln_g[l, 2:3], ln_b[l, 2:3], l, 1, alpha, tm_ffn, tf)
        conv_l.append(conv_new)
        s_l.append(s_new)
    k_new, v_new = (a.reshape(depth, bsz, t, N_DIFF, DIFF_VD) for a in kv_stacks)
    return x.reshape(bsz, t, d), jnp.stack(conv_l), k_new, v_new, jnp.stack(s_l)


def _pick(n, cap):
    if n <= cap:
        return n
    best = None
    for c in range(128, cap + 1, 128):
        if n % c == 0:
            best = c
    assert best is not None
    return best


def kernel(x_prompt, x_sample, cache_conv, cache_k, cache_v, state_ret, ln_g, ln_b, ffn_w_gate, ffn_w_up,
           ffn_w_down, w_in, w_out, conv_w, conv_b, conv_ln_g, conv_ln_b, diff_lam, diff_sub_g, rel_bias):
    tf = _pick(ffn_w_gate.shape[-1], 512)
    params = (ln_g, ln_b, ffn_w_gate.astype(BF16), ffn_w_up.astype(BF16), ffn_w_down.astype(BF16),
              w_in.astype(BF16), w_out.astype(BF16), conv_w, conv_b, conv_ln_g, conv_ln_b, diff_lam, diff_sub_g,
              rel_bias)
    bp, tp, _ = x_prompt.shape
    bs, ts, _ = x_sample.shape
    tok_p = _pick(tp, 256)
    y_p, conv_p, k_p, v_p, ret_p = _trunk(
        x_prompt, 0, None, params, (_pick(bp * tp, 512), tf, tok_p, _pick(bp * tp, 512), _pick(tp, 512)))
    y_s, conv_s, k_s, v_s, ret_s = _trunk(
        x_sample, cache_k.shape[2], (cache_conv, cache_k, cache_v, state_ret), params,
        (_pick(bs * ts, 512), tf, ts, _pick(bs * ts, 512), None))
    return (y_p, y_s, conv_p, k_p, v_p, ret_p, conv_s, k_s, v_s, ret_s)
```

```python
import functools
import math

import jax
import jax.numpy as jnp
import numpy as np
from jax import lax
from jax.experimental import pallas as pl
from jax.experimental.pallas import tpu as pltpu

CHUNK = 64
CONV_WIDTH = 31
D_CONV = 512
N_DIFF = 8
DIFF_HD = 64
DIFF_VD = 2 * DIFF_HD
N_RET = 4
RET_HD = 128
N_BUCKETS = 32
MAX_DISTANCE = 128
LN_EPS = 1e-5
ROPE_BASE = 10000.0

D_ATT = N_DIFF * DIFF_VD
VT_ROWS = DIFF_VD + 16
D_RET = N_RET * RET_HD
C_CA, C_CG = 0, D_CONV
C_Q = 2 * D_CONV
C_K = C_Q + D_ATT
C_V = C_K + D_ATT
C_RQ = C_V + D_ATT
C_RK = C_RQ + D_RET
C_RV = C_RK + D_RET
C_RG = C_RV + D_RET
D_IN = C_RG + D_RET

LOG2E = math.log2(math.e)
REF_SLACK = 8.0
MASKED = -1e30
ATTN_ROW_BLOCK = 32
SUBLANES = 8
CONV_PAD = 32
V7X_VMEM_LIMIT = 56 * 1024 * 1024

F32 = jnp.float32
BF16 = jnp.bfloat16


def _cparams(sem, vmem=None):
    return pltpu.CompilerParams(dimension_semantics=sem, vmem_limit_bytes=vmem)


def _sigmoid(x):
    return 1.0 / (1.0 + jnp.exp(-x))


def _layer_norm_rows(y, g, b):
    mu = jnp.mean(y, axis=-1, keepdims=True)
    d = y - mu
    var = jnp.mean(d * d, axis=-1, keepdims=True)
    return d * lax.rsqrt(var + LN_EPS) * g + b


def _dot(a, b):
    return jnp.dot(a, b, preferred_element_type=F32)


def _dot_nt(a, b):
    return lax.dot_general(a, b, (((1,), (1,)), ((), ())), preferred_element_type=F32)


def _dot_tn(a, b):
    return lax.dot_general(a, b, (((0,), (0,)), ((), ())), preferred_element_type=F32)


def _ffn_ln_body(alpha, nf, x_ref, wg_ref, wu_ref, wd_ref, g_ref, b_ref, o_ref, xb_ref):
    f = pl.program_id(1)

    @pl.when(f == 0)
    def _():
        xb_ref[...] = x_ref[...].astype(BF16)
        o_ref[...] = jnp.zeros_like(o_ref)

    xb = xb_ref[...]
    hg = _dot(xb, wg_ref[...])
    hu = _dot(xb, wu_ref[...])
    act = (hg * _sigmoid(hg) * hu).astype(BF16)
    o_ref[...] += _dot(act, wd_ref[...])

    @pl.when(f == nf - 1)
    def _():
        y = alpha * x_ref[...] + 0.5 * o_ref[...]
        o_ref[...] = _layer_norm_rows(y, g_ref[...], b_ref[...])


def _ffn_ln(x, wg, wu, wd, g, b, layer, slot, alpha, tm, tf):
    n, d = x.shape
    nf = wg.shape[-1] // tf
    return pl.pallas_call(
        functools.partial(_ffn_ln_body, alpha, nf),
        grid=(n // tm, nf),
        in_specs=[
            pl.BlockSpec((tm, d), lambda i, f: (i, 0)),
            pl.BlockSpec((None, None, d, tf), lambda i, f: (layer, slot, 0, f)),
            pl.BlockSpec((None, None, d, tf), lambda i, f: (layer, slot, 0, f)),
            pl.BlockSpec((None, None, tf, d), lambda i, f: (layer, slot, f, 0)),
            pl.BlockSpec((1, d), lambda i, f: (0, 0)),
            pl.BlockSpec((1, d), lambda i, f: (0, 0)),
        ],
        out_specs=pl.BlockSpec((tm, d), lambda i, f: (i, 0)),
        out_shape=jax.ShapeDtypeStruct((n, d), F32),
        scratch_shapes=[pltpu.VMEM((tm, d), BF16)],
        compiler_params=_cparams(("parallel", "arbitrary"), V7X_VMEM_LIMIT),
        name="ffn_ln",
    )(x, wg, wu, wd, g, b)


def _rope_heads(x, cos, sin):
    outs = []
    for h in range(N_RET):
        xh = x[:, h * RET_HD:(h + 1) * RET_HD]
        outs.append(xh * cos + pltpu.roll(xh, RET_HD // 2, 1) * sin)
    return jnp.concatenate(outs, axis=1)


def _in_proj_body(tm, n_aliased, h_ref, w_ref, cos_ref, sin_ref, wk_ref, *refs):
    u_ref, q_ref, kf_ref, kb_ref, vf_ref, vt_ref, rq_ref, rk_ref, rkw_ref, rv_ref, rg_ref = refs[n_aliased:]
    hb = h_ref[...].astype(BF16)

    def mm(lo, hi):
        return _dot(hb, w_ref[:, lo:hi])

    u_ref[...] = mm(C_CA, C_CG) * _sigmoid(mm(C_CG, C_Q))
    q_ref[...] = (mm(C_Q, C_K) * (DIFF_HD ** -0.5 * LOG2E)).astype(BF16)
    k = mm(C_K, C_V)
    kf_ref[...] = k
    kb_ref[...] = k.astype(BF16)
    v = mm(C_V, C_RQ)
    vf_ref[...] = v
    vt_ref[:, 0:DIFF_VD, :] = v.T.reshape(N_DIFF, DIFF_VD, tm).astype(BF16)
    pad_row = lax.broadcasted_iota(jnp.int32, (N_DIFF, VT_ROWS - DIFF_VD, tm), 1)
    vt_ref[:, DIFF_VD:VT_ROWS, :] = jnp.where(pad_row == 0, 1.0, 0.0).astype(BF16)
    cos = cos_ref[...]
    sin = sin_ref[...]
    rq_ref[...] = _rope_heads(mm(C_RQ, C_RK), cos, sin).astype(BF16)
    rk = _rope_heads(mm(C_RK, C_RV), cos, sin) * (RET_HD ** -0.5)
    rk_ref[...] = rk.astype(BF16)
    rkw_ref[...] = (rk * wk_ref[...]).astype(BF16)
    rv_ref[...] = mm(C_RV, C_RG).astype(BF16)
    rg = mm(C_RG, D_IN)
    rg_ref[...] = rg * _sigmoid(rg)


def _in_proj(h, w_in, layer, depth, kv_stacks, cos, sin, wk_tab, bsz, t, tm):
    n, d = h.shape
    nt = t // tm
    row = lambda b, i: (b * nt + i, 0)
    STACK, VT = "stack", "vt"
    outs = [
        ((n, D_CONV), F32),
        ((n, D_ATT), BF16),
        STACK,
        ((n, D_ATT), BF16),
        STACK,
        VT,
        ((n, D_RET), BF16),
        ((n, D_RET), BF16),
        ((n, D_RET), BF16),
        ((n, D_RET), BF16),
        ((n, D_RET), F32),
    ]
    out_shape, out_specs = [], []
    for o in outs:
        if o is VT:
            out_shape.append(jax.ShapeDtypeStruct((bsz, N_DIFF, nt, VT_ROWS, tm), BF16))
            out_specs.append(pl.BlockSpec((None, N_DIFF, None, VT_ROWS, tm), lambda b, i: (b, 0, i, 0, 0)))
        elif o is STACK:
            out_shape.append(jax.ShapeDtypeStruct((depth, n, D_ATT), F32))
            out_specs.append(pl.BlockSpec((None, tm, D_ATT), lambda b, i: (layer, b * nt + i, 0)))
        else:
            out_shape.append(jax.ShapeDtypeStruct(*o))
            out_specs.append(pl.BlockSpec((tm, o[0][1]), row))
    in_specs = [
        pl.BlockSpec((tm, d), row),
        pl.BlockSpec((None, d, D_IN), lambda b, i: (layer, 0, 0), pipeline_mode=pl.Buffered(1)),
        pl.BlockSpec((tm, RET_HD), lambda b, i: (i, 0)),
        pl.BlockSpec((tm, RET_HD), lambda b, i: (i, 0)),
        pl.BlockSpec((tm, D_RET), lambda b, i: (0, 0)),
    ]
    operands = [h, w_in, cos, sin, wk_tab]
    aliases = {}
    if kv_stacks is not None:
        for stack, out_idx in zip(kv_stacks, [k for k, o in enumerate(outs) if o is STACK]):
            aliases[len(operands)] = out_idx
            in_specs.append(pl.BlockSpec(memory_space=pl.ANY))
            operands.append(stack)
    return pl.pallas_call(
        functools.partial(_in_proj_body, tm, len(aliases)),
        grid=(bsz, nt),
        in_specs=in_specs,
        out_specs=out_specs,
        out_shape=out_shape,
        input_output_aliases=aliases,
        compiler_params=_cparams(("parallel", "parallel"), V7X_VMEM_LIMIT),
        name="in_proj",
    )(*operands)


def _rel_bucket(rel):
    half = N_BUCKETS // 2
    max_exact = half // 2
    ret = jnp.where(rel > 0, half, 0)
    n = jnp.abs(rel)
    nf = jnp.maximum(n, 1).astype(F32)
    large = max_exact + (jnp.log(nf / max_exact) / math.log(MAX_DISTANCE / max_exact)
                         * (half - max_exact)).astype(jnp.int32)
    large = jnp.minimum(large, half - 1)
    return ret + jnp.where(n < max_exact, n, large)


def _toeplitz_bias(rel_bias, rel_of_diff, rows, cols):
    period = rows + cols
    x = np.arange(period)
    diff = np.where(x < cols, x, x - period)
    rel = jnp.asarray(rel_of_diff(diff), jnp.int32)
    wp = rel_bias[_rel_bucket(rel)].astype(F32).T
    flat = jnp.tile(wp, (1, rows))[:, :rows * (period - 1)]
    return flat.reshape(-1, rows, period - 1)[:, :, :cols]


def _lam_value(lam_ref, lam_init):
    lf = lam_ref[...]
    a = jnp.sum(lf[0:1, :] * lf[1:2, :], axis=1, keepdims=True)
    b = jnp.sum(lf[2:3, :] * lf[3:4, :], axis=1, keepdims=True)
    return jnp.exp(a) - jnp.exp(b) + lam_init


def _block_diag_q(q):
    lane = lax.broadcasted_iota(jnp.int32, q.shape, 1)
    zero = jnp.zeros_like(q)
    return jnp.concatenate([jnp.where(lane < DIFF_HD, q, zero), jnp.where(lane >= DIFF_HD, q, zero)], axis=0)


def _attn_prompt_body(t, tv, nq, lam_init, q_ref, k_ref, vt_ref, bias_ref, lam_ref, subg_ref, o_ref,
                      qbd_ref, pa_ref, pb_ref, r_ref, g_ref, acc_ref):
    i = pl.program_id(2)
    sub = t // tv
    rb = ATTN_ROW_BLOCK
    lanes = 2 * t
    q_t = q_ref[...].astype(F32).T.astype(BF16)
    head_row = lax.broadcasted_iota(jnp.int32, q_t.shape, 0)
    zero = jnp.zeros_like(q_t)
    qbd_ref[:, 0:t] = jnp.where(head_row < DIFF_HD, q_t, zero)
    qbd_ref[:, t:lanes] = jnp.where(head_row >= DIFF_HD, q_t, zero)

    def tile_scores(j, bias_idx):
        k = k_ref[pl.ds(pl.multiple_of(j * t, t), t), :]
        s = _dot(k, qbd_ref[...])
        if bias_idx is not None:
            bias = bias_ref[bias_idx]
            s = s + jnp.concatenate([bias, bias], axis=1)
        return s

    def column_max(s):
        return jnp.max(s.reshape(t // 8, 8, lanes), axis=0)

    def step(j, bias_idx, p_ref, s=None):
        if s is None:
            s = tile_scores(j, bias_idx)
        g_ref[...] = jnp.maximum(g_ref[...], column_max(s))
        r = r_ref[...]
        for c in range(t // rb):
            p_ref[c * rb:(c + 1) * rb, :] = jnp.exp2((s[c * rb:(c + 1) * rb, :] - r).astype(BF16))
        pv = None
        for u in range(sub):
            part = _dot(vt_ref[j * sub + u], p_ref[u * tv:(u + 1) * tv, :])
            pv = part if pv is None else pv + part
        acc_ref[...] += pv

    def sweep(reference_from_diagonal):
        acc_ref[...] = jnp.zeros_like(acc_ref)
        g_ref[...] = jnp.full(g_ref.shape, MASKED, F32)

        def diagonal_scores():
            s = tile_scores(i, 0)
            if reference_from_diagonal:
                r_ref[...] = jnp.max(column_max(s), axis=0, keepdims=True)
            return s

        @pl.when(i == 0)
        def _():
            step(i, 0, pa_ref, diagonal_scores())

        @pl.when(i >= 1)
        def _():
            s_b = tile_scores(i - 1, 1)
            s_a = diagonal_scores()
            step(i, 0, pa_ref, s_a)
            step(i - 1, 1, pb_ref, s_b)

        n_far = jnp.maximum(i - 1, 0)
        rem = n_far % 4

        @pl.when(rem % 2 == 1)
        def _():
            step(i - 2, None, pa_ref)

        @pl.when(rem >= 2)
        def _():
            j = i - 2 - rem % 2
            s_b = tile_scores(j - 1, None)
            step(j, None, pa_ref)
            step(j - 1, None, pb_ref, s_b)

        def far_quad(c, carry):
            j = i - 2 - rem - 4 * c
            s_a = tile_scores(j, None)
            s_b = tile_scores(j - 1, None)
            s_c = tile_scores(j - 2, None)
            step(j, None, pa_ref, s_a)
            s_d = tile_scores(j - 3, None)
            step(j - 1, None, pb_ref, s_b)
            step(j - 2, None, pa_ref, s_c)
            step(j - 3, None, pb_ref, s_d)
            return carry

        lax.fori_loop(0, n_far // 4, far_quad, 0)

    sweep(True)
    true_max = jnp.max(g_ref[...], axis=0, keepdims=True)

    @pl.when(jnp.max(true_max - r_ref[...]) > REF_SLACK)
    def _():
        r_ref[...] = true_max
        sweep(False)

    acc = acc_ref[...]
    o = acc[0:DIFF_VD, :] / acc[DIFF_VD:DIFF_VD + 1, :]
    lam = _lam_value(lam_ref, lam_init)
    a = o[:, :t] - lam * o[:, t:]
    a = a * lax.rsqrt(jnp.mean(a * a, axis=0, keepdims=True) + LN_EPS)
    o_ref[...] = (a.T * subg_ref[...] * (1.0 - lam_init)).astype(o_ref.dtype)


def _attn_prompt(q, kb, vt, rel_bias, diff_lam, sub_g, layer, lam_init, bsz, seq, t):
    tv = vt.shape[-1]
    assert t >= MAX_DISTANCE and t % CHUNK == 0 and seq % t == 0 and t % tv == 0
    n = q.shape[0]
    nq = seq // t
    far_bias = rel_bias[_rel_bucket(jnp.full((), -MAX_DISTANCE, jnp.int32))].astype(F32)[:, None, None]
    b0 = _toeplitz_bias(rel_bias, lambda d: -d, t, t) - far_bias
    b1 = _toeplitz_bias(rel_bias, lambda d: -d - t, t, t) - far_bias
    r = jnp.arange(t, dtype=jnp.int32)[:, None]
    c = jnp.arange(t, dtype=jnp.int32)[None, :]
    b0 = jnp.where((r // CHUNK) <= (c // CHUNK), b0 * LOG2E, MASKED)
    tiles = jnp.stack([b0, b1 * LOG2E], axis=1)

    return pl.pallas_call(
        functools.partial(_attn_prompt_body, t, tv, nq, lam_init),
        grid=(bsz, N_DIFF, nq),
        in_specs=[
            pl.BlockSpec((t, DIFF_VD), lambda b, h, i: (b * nq + i, h)),
            pl.BlockSpec((seq, DIFF_VD), lambda b, h, i: (b, h)),
            pl.BlockSpec((None, None, seq // tv, VT_ROWS, tv), lambda b, h, i: (b, h, 0, 0, 0)),
            pl.BlockSpec((None, 2, t, t), lambda b, h, i: (h, 0, 0, 0), pipeline_mode=pl.Buffered(1)),
            pl.BlockSpec((None, 4, DIFF_HD), lambda b, h, i: (layer, 0, 0)),
            pl.BlockSpec((None, 1, DIFF_VD), lambda b, h, i: (layer, 0, 0)),
        ],
        out_specs=pl.BlockSpec((t, DIFF_VD), lambda b, h, i: (b * nq + i, h)),
        out_shape=jax.ShapeDtypeStruct((n, D_ATT), BF16),
        scratch_shapes=[
            pltpu.VMEM((DIFF_VD, 2 * t), BF16),
            pltpu.VMEM((t, 2 * t), BF16),
            pltpu.VMEM((t, 2 * t), BF16),
            pltpu.VMEM((1, 2 * t), F32),
            pltpu.VMEM((8, 2 * t), F32),
            pltpu.VMEM((VT_ROWS, 2 * t), F32),
        ],
        compiler_params=_cparams(("parallel", "parallel", "arbitrary"), V7X_VMEM_LIMIT),
        name="attn_prompt",
    )(q, kb, vt, tiles, diff_lam, sub_g.reshape(-1, 1, DIFF_VD))


def _attn_sample_body(tq, lam_init, q_ref, kp_ref, kn_ref, vp_ref, vn_ref, bp_ref, bn_ref, lam_ref, subg_ref, o_ref):
    qbd = _block_diag_q(q_ref[...])
    s_p = _dot_nt(qbd, kp_ref[...].astype(BF16)) + bp_ref[...]
    s_n = _dot_nt(qbd, kn_ref[...]) + bn_ref[...]
    m = jnp.maximum(jnp.max(s_p, axis=1, keepdims=True), jnp.max(s_n, axis=1, keepdims=True))
    p_p = jnp.exp2(s_p - m)
    p_n = jnp.exp2(s_n - m)
    l = jnp.sum(p_p, axis=1, keepdims=True) + jnp.sum(p_n, axis=1, keepdims=True)
    o = (_dot(p_p.astype(BF16), vp_ref[...].astype(BF16)) + _dot(p_n.astype(BF16), vn_ref[...].astype(BF16))) / l
    lam = _lam_value(lam_ref, lam_init)
    a = o[:tq, :] - lam * o[tq:, :]
    a = a * lax.rsqrt(jnp.mean(a * a, axis=1, keepdims=True) + LN_EPS)
    o_ref[...] = (a * subg_ref[...] * (1.0 - lam_init)).astype(o_ref.dtype)


def _attn_sample(q, kb, v_stack, cache_k, cache_v, rel_bias, diff_lam, sub_g, layer, lam_init, bsz, tq):
    assert tq <= 128
    n = q.shape[0]
    past = cache_k.shape[2]
    bias = _toeplitz_bias(rel_bias, lambda d: d - past, tq, past + tq) * LOG2E
    qpos = past + jnp.arange(tq, dtype=jnp.int32)[:, None]
    kpos = jnp.arange(past + tq, dtype=jnp.int32)[None, :]
    bias = jnp.where((kpos // CHUNK) <= (qpos // CHUNK), bias, MASKED)
    bias = jnp.concatenate([bias, bias], axis=1)
    cache_k = cache_k.reshape(cache_k.shape[:3] + (D_ATT,))
    cache_v = cache_v.reshape(cache_v.shape[:3] + (D_ATT,))
    cache_spec = pl.BlockSpec((None, None, past, DIFF_VD), lambda b, h: (layer, b, 0, h))
    new_spec = pl.BlockSpec((tq, DIFF_VD), lambda b, h: (b, h))
    return pl.pallas_call(
        functools.partial(_attn_sample_body, tq, lam_init),
        grid=(bsz, N_DIFF),
        in_specs=[
            new_spec, cache_spec, new_spec, cache_spec,
            pl.BlockSpec((None, tq, DIFF_VD), lambda b, h: (layer, b, h)),
            pl.BlockSpec((None, 2 * tq, past), lambda b, h: (h, 0, 0)),
            pl.BlockSpec((None, 2 * tq, tq), lambda b, h: (h, 0, 0)),
            pl.BlockSpec((None, 4, DIFF_HD), lambda b, h: (layer, 0, 0)),
            pl.BlockSpec((None, 1, DIFF_VD), lambda b, h: (layer, 0, 0)),
        ],
        out_specs=new_spec,
        out_shape=jax.ShapeDtypeStruct((n, D_ATT), BF16),
        compiler_params=_cparams(("parallel", "parallel")),
        name="attn_sample",
    )(q, cache_k, kb, cache_v, v_stack, bias[:, :, :past], bias[:, :, past:], diff_lam, sub_g.reshape(-1, 1, DIFF_VD))


def _conv_retention_body(tm, nt, u_ref, prev_ref, cw_ref, cb_ref, cg_ref, cbeta_ref,
                         q_ref, k_ref, kw_ref, v_ref, gate_ref, s0_ref, decay_ref, wq_ref, gl_ref,
                         conv_ref, convnew_ref, ret_ref, snew_ref, ext_ref, s_ref):
    i = pl.program_id(1)
    hist = CONV_WIDTH - 1

    @pl.when(i == 0)
    def _():
        ext_ref[0, 0:CONV_PAD - hist, :] = jnp.zeros((CONV_PAD - hist, D_CONV), F32)
        ext_ref[0, CONV_PAD - hist:CONV_PAD, :] = prev_ref[...]
        s_ref[...] = s0_ref[...]

    @pl.when(i > 0)
    def _():
        ext_ref[0, 0:CONV_PAD, :] = ext_ref[0, tm:tm + CONV_PAD, :]

    ext_ref[0, CONV_PAD:CONV_PAD + tm, :] = u_ref[...]
    shifted_rows = tm + CONV_PAD - SUBLANES
    for s in range(1, SUBLANES):
        ext_ref[s, 0:shifted_rows, :] = ext_ref[0, s:s + shifted_rows, :]
    rb = 32
    cb = cb_ref[...]
    cg = cg_ref[...]
    cbeta = cbeta_ref[...]
    for r0 in range(0, tm, rb):
        acc = jnp.zeros((rb, D_CONV), F32)
        for w in range(CONV_WIDTH):
            shift = (CONV_PAD - hist + w) % SUBLANES
            start = CONV_PAD - hist + w - shift + r0
            acc = acc + ext_ref[shift, start:start + rb, :] * cw_ref[w:w + 1, :]
        y = _layer_norm_rows(acc + cb, cg, cbeta)
        conv_ref[r0:r0 + rb, :] = (y * _sigmoid(y)).astype(conv_ref.dtype)

    for h in range(N_RET):
        cols = slice(h * RET_HD, (h + 1) * RET_HD)
        q = q_ref[:, cols]
        v = v_ref[:, cols]
        s_prev = s_ref[h]
        inner = _dot_nt(q, k_ref[:, cols]) * decay_ref[h]
        o = _dot(inner.astype(BF16), v) + _dot(q, s_prev.astype(BF16)) * wq_ref[:, cols]
        s_ref[h] = gl_ref[h] * s_prev + _dot_tn(kw_ref[:, cols], v)
        mu = jnp.mean(o, axis=1, keepdims=True)
        d = o - mu
        var = jnp.mean(d * d, axis=1, keepdims=True)
        ret_ref[:, cols] = (gate_ref[:, cols] * (d * lax.rsqrt(var + LN_EPS))).astype(ret_ref.dtype)

    @pl.when(i == nt - 1)
    def _():
        convnew_ref[...] = ext_ref[0, CONV_PAD + tm - hist:CONV_PAD + tm, :]
        snew_ref[...] = s_ref[...]


def _retention_tables(lc):
    log_g = jnp.log1p(-jnp.exp2(-5.0 - jnp.arange(N_RET, dtype=F32)))
    idx = jnp.arange(lc, dtype=F32)
    rel = idx[:, None] - idx[None, :]
    decay = jnp.where(rel >= 0, jnp.exp(jnp.maximum(rel, 0.0)[None] * log_g[:, None, None]), 0.0)
    w_k = jnp.exp((lc - 1.0 - idx)[None, :] * log_g[:, None])
    w_q = jnp.exp((idx + 1.0)[None, :] * log_g[:, None])
    g_l = jnp.exp(lc * log_g)
    widen = lambda w: jnp.repeat(w.T, RET_HD, axis=1)
    gl_tab = jnp.broadcast_to(g_l[:, None, None], (N_RET, 1, RET_HD))
    return decay, widen(w_q), widen(w_k), gl_tab


def _conv_retention(u, conv_prev, conv_w, conv_b, conv_g, conv_beta, rq, rk, rkw, rv, gate, s0, decay, wq_tab, gl_tab,
                    layer, bsz, t, tm):
    n = u.shape[0]
    nt = t // tm
    assert t >= CONV_PAD and tm >= CONV_PAD and tm % 32 == 0
    row = lambda b, i: (b * nt + i, 0)
    tile = pl.BlockSpec((tm, D_RET), row)
    conv_tile = pl.BlockSpec((tm, D_CONV), row)
    conv_hist = pl.BlockSpec((None, CONV_WIDTH - 1, D_CONV), lambda b, i: (b, 0, 0))
    conv_vec = pl.BlockSpec((None, 1, D_CONV), lambda b, i: (layer, 0, 0))
    state = pl.BlockSpec((None, N_RET, RET_HD, RET_HD), lambda b, i: (b, 0, 0, 0))
    return pl.pallas_call(
        functools.partial(_conv_retention_body, tm, nt),
        grid=(bsz, nt),
        in_specs=[
            conv_tile, conv_hist,
            pl.BlockSpec((None, CONV_WIDTH, D_CONV), lambda b, i: (layer, 0, 0)),
            conv_vec, conv_vec, conv_vec,
            tile, tile, tile, tile, tile, state,
            pl.BlockSpec((N_RET, tm, tm), lambda b, i: (0, 0, 0)),
            pl.BlockSpec((tm, D_RET), lambda b, i: (0, 0)),
            pl.BlockSpec((N_RET, 1, RET_HD), lambda b, i: (0, 0, 0)),
        ],
        out_specs=[conv_tile, conv_hist, tile, state],
        out_shape=[
            jax.ShapeDtypeStruct((n, D_CONV), BF16),
            jax.ShapeDtypeStruct((bsz, CONV_WIDTH - 1, D_CONV), F32),
            jax.ShapeDtypeStruct((n, D_RET), BF16),
            jax.ShapeDtypeStruct((bsz, N_RET, RET_HD, RET_HD), F32),
        ],
        scratch_shapes=[pltpu.VMEM((SUBLANES, tm + CONV_PAD, D_CONV), F32), pltpu.VMEM((N_RET, RET_HD, RET_HD), F32)],
        compiler_params=_cparams(("parallel", "arbitrary")),
        name="conv_retention",
    )(u, conv_prev, conv_w, conv_b.reshape(-1, 1, D_CONV), conv_g.reshape(-1, 1, D_CONV),
      conv_beta.reshape(-1, 1, D_CONV), rq, rk, rkw, rv, gate, s0, decay, wq_tab, gl_tab)


def _out_proj_ln_body(alpha, x_ref, c_ref, a_ref, r_ref, w_ref, g_ref, b_ref, o_ref):
    e0 = D_CONV
    e1 = D_CONV + D_ATT
    mix = (_dot(c_ref[...], w_ref[0:e0, :]) + _dot(a_ref[...], w_ref[e0:e1, :])
           + _dot(r_ref[...], w_ref[e1:e1 + D_RET, :]))
    o_ref[...] = _layer_norm_rows(alpha * x_ref[...] + mix, g_ref[...], b_ref[...])


def _out_proj_ln(x, conv_out, attn_out, ret_out, w_out, g, b, layer, alpha, tm):
    n, d = x.shape
    d_mix = D_CONV + D_ATT + D_RET
    tile = lambda w: pl.BlockSpec((tm, w), lambda i: (i, 0))
    return pl.pallas_call(
        functools.partial(_out_proj_ln_body, alpha),
        grid=(n // tm,),
        in_specs=[
            tile(d), tile(D_CONV), tile(D_ATT), tile(D_RET),
            pl.BlockSpec((None, d_mix, d), lambda i: (layer, 0, 0), pipeline_mode=pl.Buffered(1)),
            pl.BlockSpec((1, d), lambda i: (0, 0)),
            pl.BlockSpec((1, d), lambda i: (0, 0)),
        ],
        out_specs=tile(d),
        out_shape=jax.ShapeDtypeStruct((n, d), F32),
        compiler_params=_cparams(("parallel",), V7X_VMEM_LIMIT),
        name="out_proj_ln",
    )(x, conv_out, attn_out, ret_out, w_out, g, b)


def _rope_tables(pos0, t):
    half = RET_HD // 2
    inv = 1.0 / (ROPE_BASE ** (jnp.arange(half, dtype=F32) / half))
    ang = (pos0 + jnp.arange(t, dtype=jnp.int32)).astype(F32)[:, None] * inv[None, :]
    cos = jnp.cos(ang)
    sin = jnp.sin(ang)
    return jnp.concatenate([cos, cos], axis=1), jnp.concatenate([-sin, sin], axis=1)


def _trunk(x, pos0, caches, params, tiles):
    (ln_g, ln_b, wg, wu, wd, w_in, w_out, conv_w, conv_b, conv_ln_g, conv_ln_b, diff_lam, diff_sub_g,
     rel_bias) = params
    bsz, t, d = x.shape
    depth = ln_g.shape[0]
    alpha = (2 * depth) ** 0.25
    tm_ffn, tf, tm_tok, tm_out, t_att = tiles
    prompt = caches is None
    lc = tm_tok
    cos, sin = _rope_tables(pos0, t)
    decay, wq_tab, wk_tab, gl_tab = _retention_tables(lc)
    x = x.reshape(bsz * t, d)
    conv_l, s_l = [], []
    kv_stacks = None
    for l in range(depth):
        lam_init = 0.8 - 0.6 * math.exp(-0.3 * l)
        if prompt:
            conv_prev = jnp.zeros((bsz, CONV_WIDTH - 1, D_CONV), F32)
            ret_prev = jnp.zeros((bsz, N_RET, RET_HD, RET_HD), F32)
        else:
            cache_conv, cache_k, cache_v, state_ret = caches
            conv_prev, ret_prev = cache_conv[l], state_ret[l]
        x = _ffn_ln(x, wg, wu, wd, ln_g[l, 0:1], ln_b[l, 0:1], l, 0, alpha, tm_ffn, tf)
        u, q, k_stack, kb, v_stack, vt, rq, rk, rkw, rv, gate = _in_proj(
            x, w_in, l, depth, kv_stacks, cos, sin, wk_tab, bsz, t, tm_tok)
        kv_stacks = (k_stack, v_stack)
        if prompt:
            attn_out = _attn_prompt(q, kb, vt, rel_bias, diff_lam, diff_sub_g, l, lam_init, bsz, t, t_att)
        else:
            attn_out = _attn_sample(q, kb, v_stack, cache_k, cache_v, rel_bias, diff_lam, diff_sub_g, l, lam_init, bsz,
                                    t)
        conv_out, conv_new, ret_out, s_new = _conv_retention(
            u, conv_prev, conv_w, conv_b, conv_ln_g, conv_ln_b, rq, rk, rkw, rv, gate, ret_prev, decay, wq_tab, gl_tab,
            l, bsz, t, tm_tok)
        x = _out_proj_ln(x, conv_out, attn_out, ret_out, w_out, ln_g[l, 1:2], ln_b[l, 1:2], l, alpha, tm_out)
        x = _ffn_ln(x, wg, wu, wd, ln_g[l, 2:3], ln_b[l, 2:3], l, 1, alpha, tm_ffn, tf)
        conv_l.append(conv_new)
        s_l.append(s_new)
    k_new, v_new = (a.reshape(depth, bsz, t, N_DIFF, DIFF_VD) for a in kv_stacks)
    return x.reshape(bsz, t, d), jnp.stack(conv_l), k_new, v_new, jnp.stack(s_l)


def _pick(n, cap):
    if n <= cap:
        return n
    best = None
    for c in range(128, cap + 1, 128):
        if n % c == 0:
            best = c
    assert best is not None
    return best


def kernel(x_prompt, x_sample, cache_conv, cache_k, cache_v, state_ret, ln_g, ln_b, ffn_w_gate, ffn_w_up,
           ffn_w_down, w_in, w_out, conv_w, conv_b, conv_ln_g, conv_ln_b, diff_lam, diff_sub_g, rel_bias):
    tf = _pick(ffn_w_gate.shape[-1], 512)
    params = (ln_g, ln_b, ffn_w_gate.astype(BF16), ffn_w_up.astype(BF16), ffn_w_down.astype(BF16),
              w_in.astype(BF16), w_out.astype(BF16), conv_w, conv_b, conv_ln_g, conv_ln_b, diff_lam, diff_sub_g,
              rel_bias)
    bp, tp, _ = x_prompt.shape
    bs, ts, _ = x_sample.shape
    tok_p = _pick(tp, 256)
    y_p, conv_p, k_p, v_p, ret_p = _trunk(
        x_prompt, 0, None, params, (_pick(bp * tp, 512), tf, tok_p, _pick(bp * tp, 512), _pick(tp, 512)))
    y_s, conv_s, k_s, v_s, ret_s = _trunk(
        x_sample, cache_k.shape[2], (cache_conv, cache_k, cache_v, state_ret), params,
        (_pick(bs * ts, 512), tf, ts, _pick(bs * ts, 512), None))
    return (y_p, y_s, conv_p, k_p, v_p, ret_p, conv_s, k_s, v_s, ret_s)
```

```python
import functools
import math

import jax
import jax.numpy as jnp
import numpy as np
from jax import lax
from jax.experimental import pallas as pl
from jax.experimental.pallas import tpu as pltpu

CHUNK = 64
CONV_WIDTH = 31
D_CONV = 512
N_DIFF = 8
DIFF_HD = 64
DIFF_VD = 2 * DIFF_HD
N_RET = 4
RET_HD = 128
N_BUCKETS = 32
MAX_DISTANCE = 128
LN_EPS = 1e-5
ROPE_BASE = 10000.0

D_ATT = N_DIFF * DIFF_VD
VT_ROWS = DIFF_VD + 16
D_RET = N_RET * RET_HD
C_CA, C_CG = 0, D_CONV
C_Q = 2 * D_CONV
C_K = C_Q + D_ATT
C_V = C_K + D_ATT
C_RQ = C_V + D_ATT
C_RK = C_RQ + D_RET
C_RV = C_RK + D_RET
C_RG = C_RV + D_RET
D_IN = C_RG + D_RET

LOG2E = math.log2(math.e)
REF_SLACK = 8.0
MASKED = -1e30
ATTN_ROW_BLOCK = 32
SUBLANES = 8
CONV_PAD = 32
V7X_VMEM_LIMIT = 56 * 1024 * 1024

F32 = jnp.float32
BF16 = jnp.bfloat16


def _cparams(sem, vmem=None):
    return pltpu.CompilerParams(dimension_semantics=sem, vmem_limit_bytes=vmem)


def _sigmoid(x):
    return 1.0 / (1.0 + jnp.exp(-x))


def _layer_norm_rows(y, g, b):
    mu = jnp.mean(y, axis=-1, keepdims=True)
    d = y - mu
    var = jnp.mean(d * d, axis=-1, keepdims=True)
    return d * lax.rsqrt(var + LN_EPS) * g + b


def _dot(a, b):
    return jnp.dot(a, b, preferred_element_type=F32)


def _dot_nt(a, b):
    return lax.dot_general(a, b, (((1,), (1,)), ((), ())), preferred_element_type=F32)


def _dot_tn(a, b):
    return lax.dot_general(a, b, (((0,), (0,)), ((), ())), preferred_element_type=F32)


def _ffn_ln_body(alpha, nf, x_ref, wg_ref, wu_ref, wd_ref, g_ref, b_ref, o_ref, xb_ref):
    f = pl.program_id(1)

    @pl.when(f == 0)
    def _():
        xb_ref[...] = x_ref[...].astype(BF16)
        o_ref[...] = jnp.zeros_like(o_ref)

    xb = xb_ref[...]
    hg = _dot(xb, wg_ref[...])
    hu = _dot(xb, wu_ref[...])
    act = (hg * _sigmoid(hg) * hu).astype(BF16)
    o_ref[...] += _dot(act, wd_ref[...])

    @pl.when(f == nf - 1)
    def _():
        y = alpha * x_ref[...] + 0.5 * o_ref[...]
        o_ref[...] = _layer_norm_rows(y, g_ref[...], b_ref[...])


def _ffn_ln(x, wg, wu, wd, g, b, layer, slot, alpha, tm, tf):
    n, d = x.shape
    nf = wg.shape[-1] // tf
    return pl.pallas_call(
        functools.partial(_ffn_ln_body, alpha, nf),
        grid=(n // tm, nf),
        in_specs=[
            pl.BlockSpec((tm, d), lambda i, f: (i, 0)),
            pl.BlockSpec((None, None, d, tf), lambda i, f: (layer, slot, 0, f)),
            pl.BlockSpec((None, None, d, tf), lambda i, f: (layer, slot, 0, f)),
            pl.BlockSpec((None, None, tf, d), lambda i, f: (layer, slot, f, 0)),
            pl.BlockSpec((1, d), lambda i, f: (0, 0)),
            pl.BlockSpec((1, d), lambda i, f: (0, 0)),
        ],
        out_specs=pl.BlockSpec((tm, d), lambda i, f: (i, 0)),
        out_shape=jax.ShapeDtypeStruct((n, d), F32),
        scratch_shapes=[pltpu.VMEM((tm, d), BF16)],
        compiler_params=_cparams(("parallel", "arbitrary"), V7X_VMEM_LIMIT),
        name="ffn_ln",
    )(x, wg, wu, wd, g, b)


def _rope_heads(x, cos, sin):
    outs = []
    for h in range(N_RET):
        xh = x[:, h * RET_HD:(h + 1) * RET_HD]
        outs.append(xh * cos + pltpu.roll(xh, RET_HD // 2, 1) * sin)
    return jnp.concatenate(outs, axis=1)


def _in_proj_body(tm, n_aliased, h_ref, w_ref, cos_ref, sin_ref, wk_ref, *refs):
    u_ref, q_ref, kf_ref, kb_ref, vf_ref, vb_ref, vt_ref, rq_ref, rk_ref, rkw_ref, rv_ref, rg_ref = refs[n_aliased:]
    hb = h_ref[...].astype(BF16)

    def mm(lo, hi):
        return _dot(hb, w_ref[:, lo:hi])

    u_ref[...] = mm(C_CA, C_CG) * _sigmoid(mm(C_CG, C_Q))
    q_ref[...] = (mm(C_Q, C_K) * (DIFF_HD ** -0.5 * LOG2E)).astype(BF16)
    k = mm(C_K, C_V)
    kf_ref[...] = k.reshape(tm, N_DIFF, DIFF_VD)
    kb_ref[...] = k.astype(BF16)
    v = mm(C_V, C_RQ)
    vf_ref[...] = v.reshape(tm, N_DIFF, DIFF_VD)
    vb_ref[...] = v.astype(BF16)
    vt_ref[:, 0:DIFF_VD, :] = v.T.reshape(N_DIFF, DIFF_VD, tm).astype(BF16)
    pad_row = lax.broadcasted_iota(jnp.int32, (N_DIFF, VT_ROWS - DIFF_VD, tm), 1)
    vt_ref[:, DIFF_VD:VT_ROWS, :] = jnp.where(pad_row == 0, 1.0, 0.0).astype(BF16)
    cos = cos_ref[...]
    sin = sin_ref[...]
    rq_ref[...] = _rope_heads(mm(C_RQ, C_RK), cos, sin).astype(BF16)
    rk = _rope_heads(mm(C_RK, C_RV), cos, sin) * (RET_HD ** -0.5)
    rk_ref[...] = rk.astype(BF16)
    rkw_ref[...] = (rk * wk_ref[...]).astype(BF16)
    rv_ref[...] = mm(C_RV, C_RG).astype(BF16)
    rg = mm(C_RG, D_IN)
    rg_ref[...] = rg * _sigmoid(rg)


def _in_proj(h, w_in, layer, depth, kv_stacks, cos, sin, wk_tab, bsz, t, tm):
    n, d = h.shape
    nt = t // tm
    row = lambda b, i: (b * nt + i, 0)
    STACK, VT = "stack", "vt"
    outs = [
        ((n, D_CONV), F32),
        ((n, D_ATT), BF16),
        STACK,
        ((n, D_ATT), BF16),
        STACK,
        ((n, D_ATT), BF16),
        VT,
        ((n, D_RET), BF16),
        ((n, D_RET), BF16),
        ((n, D_RET), BF16),
        ((n, D_RET), BF16),
        ((n, D_RET), F32),
    ]
    out_shape, out_specs = [], []
    for o in outs:
        if o is VT:
            out_shape.append(jax.ShapeDtypeStruct((bsz, N_DIFF, nt, VT_ROWS, tm), BF16))
            out_specs.append(pl.BlockSpec((None, N_DIFF, None, VT_ROWS, tm), lambda b, i: (b, 0, i, 0, 0)))
        elif o is STACK:
            out_shape.append(jax.ShapeDtypeStruct((depth, n, N_DIFF, DIFF_VD), F32))
            out_specs.append(pl.BlockSpec((None, tm, N_DIFF, DIFF_VD), lambda b, i: (layer, b * nt + i, 0, 0)))
        else:
            out_shape.append(jax.ShapeDtypeStruct(*o))
            out_specs.append(pl.BlockSpec((tm, o[0][1]), row))
    in_specs = [
        pl.BlockSpec((tm, d), row),
        pl.BlockSpec((None, d, D_IN), lambda b, i: (layer, 0, 0), pipeline_mode=pl.Buffered(1)),
        pl.BlockSpec((tm, RET_HD), lambda b, i: (i, 0)),
        pl.BlockSpec((tm, RET_HD), lambda b, i: (i, 0)),
        pl.BlockSpec((tm, D_RET), lambda b, i: (0, 0)),
    ]
    operands = [h, w_in, cos, sin, wk_tab]
    aliases = {}
    if kv_stacks is not None:
        for stack, out_idx in zip(kv_stacks, [k for k, o in enumerate(outs) if o is STACK]):
            aliases[len(operands)] = out_idx
            in_specs.append(pl.BlockSpec(memory_space=pl.ANY))
            operands.append(stack)
    return pl.pallas_call(
        functools.partial(_in_proj_body, tm, len(aliases)),
        grid=(bsz, nt),
        in_specs=in_specs,
        out_specs=out_specs,
        out_shape=out_shape,
        input_output_aliases=aliases,
        compiler_params=_cparams(("parallel", "parallel"), V7X_VMEM_LIMIT),
        name="in_proj",
    )(*operands)


def _rel_bucket(rel):
    half = N_BUCKETS // 2
    max_exact = half // 2
    ret = jnp.where(rel > 0, half, 0)
    n = jnp.abs(rel)
    nf = jnp.maximum(n, 1).astype(F32)
    large = max_exact + (jnp.log(nf / max_exact) / math.log(MAX_DISTANCE / max_exact)
                         * (half - max_exact)).astype(jnp.int32)
    large = jnp.minimum(large, half - 1)
    return ret + jnp.where(n < max_exact, n, large)


def _toeplitz_bias(rel_bias, rel_of_diff, rows, cols):
    period = rows + cols
    x = np.arange(period)
    diff = np.where(x < cols, x, x - period)
    rel = jnp.asarray(rel_of_diff(diff), jnp.int32)
    wp = rel_bias[_rel_bucket(rel)].astype(F32).T
    flat = jnp.tile(wp, (1, rows))[:, :rows * (period - 1)]
    return flat.reshape(-1, rows, period - 1)[:, :, :cols]


def _lam_value(lam_ref, lam_init):
    lf = lam_ref[...]
    a = jnp.sum(lf[0:1, :] * lf[1:2, :], axis=1, keepdims=True)
    b = jnp.sum(lf[2:3, :] * lf[3:4, :], axis=1, keepdims=True)
    return jnp.exp(a) - jnp.exp(b) + lam_init


def _block_diag_q(q):
    lane = lax.broadcasted_iota(jnp.int32, q.shape, 1)
    zero = jnp.zeros_like(q)
    return jnp.concatenate([jnp.where(lane < DIFF_HD, q, zero), jnp.where(lane >= DIFF_HD, q, zero)], axis=0)


def _attn_prompt_body(t, tv, nq, lam_init, q_ref, k_ref, vt_ref, bias_ref, lam_ref, subg_ref, o_ref,
                      qbd_ref, pa_ref, pb_ref, r_ref, g_ref, acc_ref):
    i = pl.program_id(2)
    sub = t // tv
    rb = ATTN_ROW_BLOCK
    lanes = 2 * t
    q_t = q_ref[...].astype(F32).T.astype(BF16)
    head_row = lax.broadcasted_iota(jnp.int32, q_t.shape, 0)
    zero = jnp.zeros_like(q_t)
    qbd_ref[:, 0:t] = jnp.where(head_row < DIFF_HD, q_t, zero)
    qbd_ref[:, t:lanes] = jnp.where(head_row >= DIFF_HD, q_t, zero)

    def tile_scores(j, bias_idx):
        k = k_ref[pl.ds(pl.multiple_of(j * t, t), t), :]
        s = _dot(k, qbd_ref[...])
        if bias_idx is not None:
            bias = bias_ref[bias_idx]
            s = s + jnp.concatenate([bias, bias], axis=1)
        return s

    def column_max(s):
        return jnp.max(s.reshape(t // 8, 8, lanes), axis=0)

    def step(j, bias_idx, p_ref, s=None):
        if s is None:
            s = tile_scores(j, bias_idx)
        g_ref[...] = jnp.maximum(g_ref[...], column_max(s))
        r = r_ref[...]
        for c in range(t // rb):
            p_ref[c * rb:(c + 1) * rb, :] = jnp.exp2((s[c * rb:(c + 1) * rb, :] - r).astype(BF16))
        pv = None
        for u in range(sub):
            part = _dot(vt_ref[j * sub + u], p_ref[u * tv:(u + 1) * tv, :])
            pv = part if pv is None else pv + part
        acc_ref[...] += pv

    def sweep(reference_from_diagonal):
        acc_ref[...] = jnp.zeros_like(acc_ref)
        g_ref[...] = jnp.full(g_ref.shape, MASKED, F32)

        def diagonal_scores():
            s = tile_scores(i, 0)
            if reference_from_diagonal:
                r_ref[...] = jnp.max(column_max(s), axis=0, keepdims=True)
            return s

        @pl.when(i == 0)
        def _():
            step(i, 0, pa_ref, diagonal_scores())

        @pl.when(i >= 1)
        def _():
            s_b = tile_scores(i - 1, 1)
            s_a = diagonal_scores()
            step(i, 0, pa_ref, s_a)
            step(i - 1, 1, pb_ref, s_b)

        n_far = jnp.maximum(i - 1, 0)
        rem = n_far % 4

        @pl.when(rem % 2 == 1)
        def _():
            step(i - 2, None, pa_ref)

        @pl.when(rem >= 2)
        def _():
            j = i - 2 - rem % 2
            s_b = tile_scores(j - 1, None)
            step(j, None, pa_ref)
            step(j - 1, None, pb_ref, s_b)

        def far_quad(c, carry):
            j = i - 2 - rem - 4 * c
            s_a = tile_scores(j, None)
            s_b = tile_scores(j - 1, None)
            s_c = tile_scores(j - 2, None)
            step(j, None, pa_ref, s_a)
            s_d = tile_scores(j - 3, None)
            step(j - 1, None, pb_ref, s_b)
            step(j - 2, None, pa_ref, s_c)
            step(j - 3, None, pb_ref, s_d)
            return carry

        lax.fori_loop(0, n_far // 4, far_quad, 0)

    sweep(True)
    true_max = jnp.max(g_ref[...], axis=0, keepdims=True)

    @pl.when(jnp.max(true_max - r_ref[...]) > REF_SLACK)
    def _():
        r_ref[...] = true_max
        sweep(False)

    acc = acc_ref[...]
    o = acc[0:DIFF_VD, :] / acc[DIFF_VD:DIFF_VD + 1, :]
    lam = _lam_value(lam_ref, lam_init)
    a = o[:, :t] - lam * o[:, t:]
    a = a * lax.rsqrt(jnp.mean(a * a, axis=0, keepdims=True) + LN_EPS)
    o_ref[...] = (a.T * subg_ref[...] * (1.0 - lam_init)).astype(o_ref.dtype)


def _attn_prompt(q, kb, vt, rel_bias, diff_lam, sub_g, layer, lam_init, bsz, seq, t):
    tv = vt.shape[-1]
    assert t >= MAX_DISTANCE and t % CHUNK == 0 and seq % t == 0 and t % tv == 0
    n = q.shape[0]
    nq = seq // t
    far_bias = rel_bias[_rel_bucket(jnp.full((), -MAX_DISTANCE, jnp.int32))].astype(F32)[:, None, None]
    b0 = _toeplitz_bias(rel_bias, lambda d: -d, t, t) - far_bias
    b1 = _toeplitz_bias(rel_bias, lambda d: -d - t, t, t) - far_bias
    r = jnp.arange(t, dtype=jnp.int32)[:, None]
    c = jnp.arange(t, dtype=jnp.int32)[None, :]
    b0 = jnp.where((r // CHUNK) <= (c // CHUNK), b0 * LOG2E, MASKED)
    tiles = jnp.stack([b0, b1 * LOG2E], axis=1)

    return pl.pallas_call(
        functools.partial(_attn_prompt_body, t, tv, nq, lam_init),
        grid=(bsz, N_DIFF, nq),
        in_specs=[
            pl.BlockSpec((t, DIFF_VD), lambda b, h, i: (b * nq + i, h)),
            pl.BlockSpec((seq, DIFF_VD), lambda b, h, i: (b, h)),
            pl.BlockSpec((None, None, seq // tv, VT_ROWS, tv), lambda b, h, i: (b, h, 0, 0, 0)),
            pl.BlockSpec((None, 2, t, t), lambda b, h, i: (h, 0, 0, 0), pipeline_mode=pl.Buffered(1)),
            pl.BlockSpec((None, 4, DIFF_HD), lambda b, h, i: (layer, 0, 0)),
            pl.BlockSpec((None, 1, DIFF_VD), lambda b, h, i: (layer, 0, 0)),
        ],
        out_specs=pl.BlockSpec((t, DIFF_VD), lambda b, h, i: (b * nq + i, h)),
        out_shape=jax.ShapeDtypeStruct((n, D_ATT), BF16),
        scratch_shapes=[
            pltpu.VMEM((DIFF_VD, 2 * t), BF16),
            pltpu.VMEM((t, 2 * t), BF16),
            pltpu.VMEM((t, 2 * t), BF16),
            pltpu.VMEM((1, 2 * t), F32),
            pltpu.VMEM((8, 2 * t), F32),
            pltpu.VMEM((VT_ROWS, 2 * t), F32),
        ],
        compiler_params=_cparams(("parallel", "parallel", "arbitrary"), V7X_VMEM_LIMIT),
        name="attn_prompt",
    )(q, kb, vt, tiles, diff_lam, sub_g.reshape(-1, 1, DIFF_VD))


def _attn_sample_body(tq, lam_init, q_ref, kp_ref, kn_ref, vp_ref, vn_ref, bp_ref, bn_ref, lam_ref, subg_ref, o_ref):
    qbd = _block_diag_q(q_ref[...])
    s_p = _dot_nt(qbd, kp_ref[...].astype(BF16)) + bp_ref[...]
    s_n = _dot_nt(qbd, kn_ref[...]) + bn_ref[...]
    m = jnp.maximum(jnp.max(s_p, axis=1, keepdims=True), jnp.max(s_n, axis=1, keepdims=True))
    p_p = jnp.exp2(s_p - m)
    p_n = jnp.exp2(s_n - m)
    l = jnp.sum(p_p, axis=1, keepdims=True) + jnp.sum(p_n, axis=1, keepdims=True)
    o = (_dot(p_p.astype(BF16), vp_ref[...].astype(BF16)) + _dot(p_n.astype(BF16), vn_ref[...])) / l
    lam = _lam_value(lam_ref, lam_init)
    a = o[:tq, :] - lam * o[tq:, :]
    a = a * lax.rsqrt(jnp.mean(a * a, axis=1, keepdims=True) + LN_EPS)
    o_ref[...] = (a * subg_ref[...] * (1.0 - lam_init)).astype(o_ref.dtype)


def _attn_sample(q, kb, vb, cache_k, cache_v, rel_bias, diff_lam, sub_g, layer, lam_init, bsz, tq):
    assert tq <= 128
    n = q.shape[0]
    past = cache_k.shape[2]
    bias = _toeplitz_bias(rel_bias, lambda d: d - past, tq, past + tq) * LOG2E
    qpos = past + jnp.arange(tq, dtype=jnp.int32)[:, None]
    kpos = jnp.arange(past + tq, dtype=jnp.int32)[None, :]
    bias = jnp.where((kpos // CHUNK) <= (qpos // CHUNK), bias, MASKED)
    bias = jnp.concatenate([bias, bias], axis=1)
    cache_k = cache_k.reshape(cache_k.shape[:3] + (D_ATT,))
    cache_v = cache_v.reshape(cache_v.shape[:3] + (D_ATT,))
    cache_spec = pl.BlockSpec((None, None, past, DIFF_VD), lambda b, h: (layer, b, 0, h))
    new_spec = pl.BlockSpec((tq, DIFF_VD), lambda b, h: (b, h))
    return pl.pallas_call(
        functools.partial(_attn_sample_body, tq, lam_init),
        grid=(bsz, N_DIFF),
        in_specs=[
            new_spec, cache_spec, new_spec, cache_spec, new_spec,
            pl.BlockSpec((None, 2 * tq, past), lambda b, h: (h, 0, 0)),
            pl.BlockSpec((None, 2 * tq, tq), lambda b, h: (h, 0, 0)),
            pl.BlockSpec((None, 4, DIFF_HD), lambda b, h: (layer, 0, 0)),
            pl.BlockSpec((None, 1, DIFF_VD), lambda b, h: (layer, 0, 0)),
        ],
        out_specs=new_spec,
        out_shape=jax.ShapeDtypeStruct((n, D_ATT), BF16),
        compiler_params=_cparams(("parallel", "parallel")),
        name="attn_sample",
    )(q, cache_k, kb, cache_v, vb, bias[:, :, :past], bias[:, :, past:], diff_lam, sub_g.reshape(-1, 1, DIFF_VD))


def _conv_retention_body(tm, nt, u_ref, prev_ref, cw_ref, cb_ref, cg_ref, cbeta_ref,
                         q_ref, k_ref, kw_ref, v_ref, gate_ref, s0_ref, decay_ref, wq_ref, gl_ref,
                         conv_ref, convnew_ref, ret_ref, snew_ref, ext_ref, s_ref):
    i = pl.program_id(1)
    hist = CONV_WIDTH - 1

    @pl.when(i == 0)
    def _():
        ext_ref[0, 0:CONV_PAD - hist, :] = jnp.zeros((CONV_PAD - hist, D_CONV), F32)
        ext_ref[0, CONV_PAD - hist:CONV_PAD, :] = prev_ref[...]
        s_ref[...] = s0_ref[...]

    @pl.when(i > 0)
    def _():
        ext_ref[0, 0:CONV_PAD, :] = ext_ref[0, tm:tm + CONV_PAD, :]

    ext_ref[0, CONV_PAD:CONV_PAD + tm, :] = u_ref[...]
    shifted_rows = tm + CONV_PAD - SUBLANES
    for s in range(1, SUBLANES):
        ext_ref[s, 0:shifted_rows, :] = ext_ref[0, s:s + shifted_rows, :]
    rb = 32
    cb = cb_ref[...]
    cg = cg_ref[...]
    cbeta = cbeta_ref[...]
    for r0 in range(0, tm, rb):
        acc = jnp.zeros((rb, D_CONV), F32)
        for w in range(CONV_WIDTH):
            shift = (CONV_PAD - hist + w) % SUBLANES
            start = CONV_PAD - hist + w - shift + r0
            acc = acc + ext_ref[shift, start:start + rb, :] * cw_ref[w:w + 1, :]
        y = _layer_norm_rows(acc + cb, cg, cbeta)
        conv_ref[r0:r0 + rb, :] = (y * _sigmoid(y)).astype(conv_ref.dtype)

    for h in range(N_RET):
        cols = slice(h * RET_HD, (h + 1) * RET_HD)
        q = q_ref[:, cols]
        v = v_ref[:, cols]
        s_prev = s_ref[h]
        inner = _dot_nt(q, k_ref[:, cols]) * decay_ref[h]
        o = _dot(inner.astype(BF16), v) + _dot(q, s_prev.astype(BF16)) * wq_ref[:, cols]
        s_ref[h] = gl_ref[h] * s_prev + _dot_tn(kw_ref[:, cols], v)
        mu = jnp.mean(o, axis=1, keepdims=True)
        d = o - mu
        var = jnp.mean(d * d, axis=1, keepdims=True)
        ret_ref[:, cols] = (gate_ref[:, cols] * (d * lax.rsqrt(var + LN_EPS))).astype(ret_ref.dtype)

    @pl.when(i == nt - 1)
    def _():
        convnew_ref[...] = ext_ref[0, CONV_PAD + tm - hist:CONV_PAD + tm, :]
        snew_ref[...] = s_ref[...]


def _retention_tables(lc):
    log_g = jnp.log1p(-jnp.exp2(-5.0 - jnp.arange(N_RET, dtype=F32)))
    idx = jnp.arange(lc, dtype=F32)
    rel = idx[:, None] - idx[None, :]
    decay = jnp.where(rel >= 0, jnp.exp(jnp.maximum(rel, 0.0)[None] * log_g[:, None, None]), 0.0)
    w_k = jnp.exp((lc - 1.0 - idx)[None, :] * log_g[:, None])
    w_q = jnp.exp((idx + 1.0)[None, :] * log_g[:, None])
    g_l = jnp.exp(lc * log_g)
    widen = lambda w: jnp.repeat(w.T, RET_HD, axis=1)
    gl_tab = jnp.broadcast_to(g_l[:, None, None], (N_RET, 1, RET_HD))
    return decay, widen(w_q), widen(w_k), gl_tab


def _conv_retention(u, conv_prev, conv_w, conv_b, conv_g, conv_beta, rq, rk, rkw, rv, gate, s0, decay, wq_tab, gl_tab,
                    layer, bsz, t, tm):
    n = u.shape[0]
    nt = t // tm
    assert t >= CONV_PAD and tm >= CONV_PAD and tm % 32 == 0
    row = lambda b, i: (b * nt + i, 0)
    tile = pl.BlockSpec((tm, D_RET), row)
    conv_tile = pl.BlockSpec((tm, D_CONV), row)
    conv_hist = pl.BlockSpec((None, CONV_WIDTH - 1, D_CONV), lambda b, i: (b, 0, 0))
    conv_vec = pl.BlockSpec((None, 1, D_CONV), lambda b, i: (layer, 0, 0))
    state = pl.BlockSpec((None, N_RET, RET_HD, RET_HD), lambda b, i: (b, 0, 0, 0))
    return pl.pallas_call(
        functools.partial(_conv_retention_body, tm, nt),
        grid=(bsz, nt),
        in_specs=[
            conv_tile, conv_hist,
            pl.BlockSpec((None, CONV_WIDTH, D_CONV), lambda b, i: (layer, 0, 0)),
            conv_vec, conv_vec, conv_vec,
            tile, tile, tile, tile, tile, state,
            pl.BlockSpec((N_RET, tm, tm), lambda b, i: (0, 0, 0)),
            pl.BlockSpec((tm, D_RET), lambda b, i: (0, 0)),
            pl.BlockSpec((N_RET, 1, RET_HD), lambda b, i: (0, 0, 0)),
        ],
        out_specs=[conv_tile, conv_hist, tile, state],
        out_shape=[
            jax.ShapeDtypeStruct((n, D_CONV), BF16),
            jax.ShapeDtypeStruct((bsz, CONV_WIDTH - 1, D_CONV), F32),
            jax.ShapeDtypeStruct((n, D_RET), BF16),
            jax.ShapeDtypeStruct((bsz, N_RET, RET_HD, RET_HD), F32),
        ],
        scratch_shapes=[pltpu.VMEM((SUBLANES, tm + CONV_PAD, D_CONV), F32), pltpu.VMEM((N_RET, RET_HD, RET_HD), F32)],
        compiler_params=_cparams(("parallel", "arbitrary")),
        name="conv_retention",
    )(u, conv_prev, conv_w, conv_b.reshape(-1, 1, D_CONV), conv_g.reshape(-1, 1, D_CONV),
      conv_beta.reshape(-1, 1, D_CONV), rq, rk, rkw, rv, gate, s0, decay, wq_tab, gl_tab)


def _out_proj_ln_body(alpha, x_ref, c_ref, a_ref, r_ref, w_ref, g_ref, b_ref, o_ref):
    e0 = D_CONV
    e1 = D_CONV + D_ATT
    mix = (_dot(c_ref[...], w_ref[0:e0, :]) + _dot(a_ref[...], w_ref[e0:e1, :])
           + _dot(r_ref[...], w_ref[e1:e1 + D_RET, :]))
    o_ref[...] = _layer_norm_rows(alpha * x_ref[...] + mix, g_ref[...], b_ref[...])


def _out_proj_ln(x, conv_out, attn_out, ret_out, w_out, g, b, layer, alpha, tm):
    n, d = x.shape
    d_mix = D_CONV + D_ATT + D_RET
    tile = lambda w: pl.BlockSpec((tm, w), lambda i: (i, 0))
    return pl.pallas_call(
        functools.partial(_out_proj_ln_body, alpha),
        grid=(n // tm,),
        in_specs=[
            tile(d), tile(D_CONV), tile(D_ATT), tile(D_RET),
            pl.BlockSpec((None, d_mix, d), lambda i: (layer, 0, 0), pipeline_mode=pl.Buffered(1)),
            pl.BlockSpec((1, d), lambda i: (0, 0)),
            pl.BlockSpec((1, d), lambda i: (0, 0)),
        ],
        out_specs=tile(d),
        out_shape=jax.ShapeDtypeStruct((n, d), F32),
        compiler_params=_cparams(("parallel",), V7X_VMEM_LIMIT),
        name="out_proj_ln",
    )(x, conv_out, attn_out, ret_out, w_out, g, b)


def _rope_tables(pos0, t):
    half = RET_HD // 2
    inv = 1.0 / (ROPE_BASE ** (jnp.arange(half, dtype=F32) / half))
    ang = (pos0 + jnp.arange(t, dtype=jnp.int32)).astype(F32)[:, None] * inv[None, :]
    cos = jnp.cos(ang)
    sin = jnp.sin(ang)
    return jnp.concatenate([cos, cos], axis=1), jnp.concatenate([-sin, sin], axis=1)


def _trunk(x, pos0, caches, params, tiles):
    (ln_g, ln_b, wg, wu, wd, w_in, w_out, conv_w, conv_b, conv_ln_g, conv_ln_b, diff_lam, diff_sub_g,
     rel_bias) = params
    bsz, t, d = x.shape
    depth = ln_g.shape[0]
    alpha = (2 * depth) ** 0.25
    tm_ffn, tf, tm_tok, tm_out, t_att = tiles
    prompt = caches is None
    lc = tm_tok
    cos, sin = _rope_tables(pos0, t)
    decay, wq_tab, wk_tab, gl_tab = _retention_tables(lc)
    x = x.reshape(bsz * t, d)
    conv_l, s_l = [], []
    kv_stacks = None
    for l in range(depth):
        lam_init = 0.8 - 0.6 * math.exp(-0.3 * l)
        if prompt:
            conv_prev = jnp.zeros((bsz, CONV_WIDTH - 1, D_CONV), F32)
            ret_prev = jnp.zeros((bsz, N_RET, RET_HD, RET_HD), F32)
        else:
            cache_conv, cache_k, cache_v, state_ret = caches
            conv_prev, ret_prev = cache_conv[l], state_ret[l]
        x = _ffn_ln(x, wg, wu, wd, ln_g[l, 0:1], ln_b[l, 0:1], l, 0, alpha, tm_ffn, tf)
        u, q, k_stack, kb, v_stack, vb, vt, rq, rk, rkw, rv, gate = _in_proj(
            x, w_in, l, depth, kv_stacks, cos, sin, wk_tab, bsz, t, tm_tok)
        kv_stacks = (k_stack, v_stack)
        if prompt:
            attn_out = _attn_prompt(q, kb, vt, rel_bias, diff_lam, diff_sub_g, l, lam_init, bsz, t, t_att)
        else:
            attn_out = _attn_sample(q, kb, vb, cache_k, cache_v, rel_bias, diff_lam, diff_sub_g, l, lam_init, bsz, t)
        conv_out, conv_new, ret_out, s_new = _conv_retention(
            u, conv_prev, conv_w, conv_b, conv_ln_g, conv_ln_b, rq, rk, rkw, rv, gate, ret_prev, decay, wq_tab, gl_tab,
            l, bsz, t, tm_tok)
        x = _out_proj_ln(x, conv_out, attn_out, ret_out, w_out, ln_g[l, 1:2], ln_b[l, 1:2], l, alpha, tm_out)
        x = _ffn_ln(x, wg, wu, wd, ln_g[l, 2:3], ln_b[l, 2:3], l, 1, alpha, tm_ffn, tf)
        conv_l.append(conv_new)
        s_l.append(s_new)
    k_new, v_new = (a.reshape(depth, bsz, t, N_DIFF, DIFF_VD) for a in kv_stacks)
    return x.reshape(bsz, t, d), jnp.stack(conv_l), k_new, v_new, jnp.stack(s_l)


def _pick(n, cap):
    if n <= cap:
        return n
    best = None
    for c in range(128, cap + 1, 128):
        if n % c == 0:
            best = c
    assert best is not None
    return best


def kernel(x_prompt, x_sample, cache_conv, cache_k, cache_v, state_ret, ln_g, ln_b, ffn_w_gate, ffn_w_up,
           ffn_w_down, w_in, w_out, conv_w, conv_b, conv_ln_g, conv_ln_b, diff_lam, diff_sub_g, rel_bias):
    tf = _pick(ffn_w_gate.shape[-1], 512)
    params = (ln_g, ln_b, ffn_w_gate.astype(BF16), ffn_w_up.astype(BF16), ffn_w_down.astype(BF16),
              w_in.astype(BF16), w_out.astype(BF16), conv_w, conv_b, conv_ln_g, conv_ln_b, diff_lam, diff_sub_g,
              rel_bias)
    bp, tp, _ = x_prompt.shape
    bs, ts, _ = x_sample.shape
    tok_p = _pick(tp, 256)
    y_p, conv_p, k_p, v_p, ret_p = _trunk(
        x_prompt, 0, None, params, (_pick(bp * tp, 512), tf, tok_p, _pick(bp * tp, 512), _pick(tp, 512)))
    y_s, conv_s, k_s, v_s, ret_s = _trunk(
        x_sample, cache_k.shape[2], (cache_conv, cache_k, cache_v, state_ret), params,
        (_pick(bs * ts, 512), tf, ts, _pick(bs * ts, 512), None))
    return (y_p, y_s, conv_p, k_p, v_p, ret_p, conv_s, k_s, v_s, ret_s)
```

```python
import functools
import math

import jax
import jax.numpy as jnp
import numpy as np
from jax import lax
from jax.experimental import pallas as pl
from jax.experimental.pallas import tpu as pltpu

CHUNK = 64
CONV_WIDTH = 31
D_CONV = 512
N_DIFF = 8
DIFF_HD = 64
DIFF_VD = 2 * DIFF_HD
N_RET = 4
RET_HD = 128
N_BUCKETS = 32
MAX_DISTANCE = 128
LN_EPS = 1e-5
ROPE_BASE = 10000.0

D_ATT = N_DIFF * DIFF_VD
VT_ROWS = DIFF_VD + 16
D_RET = N_RET * RET_HD
C_CA, C_CG = 0, D_CONV
C_Q = 2 * D_CONV
C_K = C_Q + D_ATT
C_V = C_K + D_ATT
C_RQ = C_V + D_ATT
C_RK = C_RQ + D_RET
C_RV = C_RK + D_RET
C_RG = C_RV + D_RET
D_IN = C_RG + D_RET

LOG2E = math.log2(math.e)
REF_SLACK = 8.0
MASKED = -1e30
ATTN_ROW_BLOCK = 32
SUBLANES = 8
CONV_PAD = 32
V7X_VMEM_LIMIT = 56 * 1024 * 1024

F32 = jnp.float32
BF16 = jnp.bfloat16


def _cparams(sem, vmem=None):
    return pltpu.CompilerParams(dimension_semantics=sem, vmem_limit_bytes=vmem)


def _sigmoid(x):
    return 1.0 / (1.0 + jnp.exp(-x))


def _layer_norm_rows(y, g, b):
    mu = jnp.mean(y, axis=-1, keepdims=True)
    d = y - mu
    var = jnp.mean(d * d, axis=-1, keepdims=True)
    return d * lax.rsqrt(var + LN_EPS) * g + b


def _dot(a, b):
    return jnp.dot(a, b, preferred_element_type=F32)


def _dot_nt(a, b):
    return lax.dot_general(a, b, (((1,), (1,)), ((), ())), preferred_element_type=F32)


def _dot_tn(a, b):
    return lax.dot_general(a, b, (((0,), (0,)), ((), ())), preferred_element_type=F32)


def _ffn_ln_body(alpha, nf, x_ref, wg_ref, wu_ref, wd_ref, g_ref, b_ref, o_ref, xb_ref):
    f = pl.program_id(1)

    @pl.when(f == 0)
    def _():
        xb_ref[...] = x_ref[...].astype(BF16)
        o_ref[...] = jnp.zeros_like(o_ref)

    xb = xb_ref[...]
    hg = _dot(xb, wg_ref[...])
    hu = _dot(xb, wu_ref[...])
    act = (hg * _sigmoid(hg) * hu).astype(BF16)
    o_ref[...] += _dot(act, wd_ref[...])

    @pl.when(f == nf - 1)
    def _():
        y = alpha * x_ref[...] + 0.5 * o_ref[...]
        o_ref[...] = _layer_norm_rows(y, g_ref[...], b_ref[...])


def _ffn_ln(x, wg, wu, wd, g, b, layer, slot, alpha, tm, tf):
    n, d = x.shape
    nf = wg.shape[-1] // tf
    return pl.pallas_call(
        functools.partial(_ffn_ln_body, alpha, nf),
        grid=(n // tm, nf),
        in_specs=[
            pl.BlockSpec((tm, d), lambda i, f: (i, 0)),
            pl.BlockSpec((None, None, d, tf), lambda i, f: (layer, slot, 0, f)),
            pl.BlockSpec((None, None, d, tf), lambda i, f: (layer, slot, 0, f)),
            pl.BlockSpec((None, None, tf, d), lambda i, f: (layer, slot, f, 0)),
            pl.BlockSpec((1, d), lambda i, f: (0, 0)),
            pl.BlockSpec((1, d), lambda i, f: (0, 0)),
        ],
        out_specs=pl.BlockSpec((tm, d), lambda i, f: (i, 0)),
        out_shape=jax.ShapeDtypeStruct((n, d), F32),
        scratch_shapes=[pltpu.VMEM((tm, d), BF16)],
        compiler_params=_cparams(("parallel", "arbitrary"), V7X_VMEM_LIMIT),
        name="ffn_ln",
    )(x, wg, wu, wd, g, b)


def _rope_heads(x, cos, sin):
    outs = []
    for h in range(N_RET):
        xh = x[:, h * RET_HD:(h + 1) * RET_HD]
        outs.append(xh * cos + pltpu.roll(xh, RET_HD // 2, 1) * sin)
    return jnp.concatenate(outs, axis=1)


def _in_proj_body(tm, n_aliased, h_ref, w_ref, cos_ref, sin_ref, wk_ref, *refs):
    u_ref, q_ref, kf_ref, kb_ref, vf_ref, vb_ref, vt_ref, rq_ref, rk_ref, rkw_ref, rv_ref, rg_ref = refs[n_aliased:]
    hb = h_ref[...].astype(BF16)

    def mm(lo, hi):
        return _dot(hb, w_ref[:, lo:hi])

    u_ref[...] = mm(C_CA, C_CG) * _sigmoid(mm(C_CG, C_Q))
    q_ref[...] = (mm(C_Q, C_K) * (DIFF_HD ** -0.5 * LOG2E)).astype(BF16)
    k = mm(C_K, C_V)
    kf_ref[...] = k.reshape(tm, N_DIFF, DIFF_VD)
    kb_ref[...] = k.astype(BF16)
    v = mm(C_V, C_RQ)
    vf_ref[...] = v.reshape(tm, N_DIFF, DIFF_VD)
    vb_ref[...] = v.astype(BF16)
    vt_ref[:, 0:DIFF_VD, :] = v.T.reshape(N_DIFF, DIFF_VD, tm).astype(BF16)
    pad_row = lax.broadcasted_iota(jnp.int32, (N_DIFF, VT_ROWS - DIFF_VD, tm), 1)
    vt_ref[:, DIFF_VD:VT_ROWS, :] = jnp.where(pad_row == 0, 1.0, 0.0).astype(BF16)
    cos = cos_ref[...]
    sin = sin_ref[...]
    rq_ref[...] = _rope_heads(mm(C_RQ, C_RK), cos, sin).astype(BF16)
    rk = _rope_heads(mm(C_RK, C_RV), cos, sin) * (RET_HD ** -0.5)
    rk_ref[...] = rk.astype(BF16)
    rkw_ref[...] = (rk * wk_ref[...]).astype(BF16)
    rv_ref[...] = mm(C_RV, C_RG).astype(BF16)
    rg = mm(C_RG, D_IN)
    rg_ref[...] = rg * _sigmoid(rg)


def _in_proj(h, w_in, layer, depth, kv_stacks, cos, sin, wk_tab, bsz, t, tm):
    n, d = h.shape
    nt = t // tm
    row = lambda b, i: (b * nt + i, 0)
    STACK, VT = "stack", "vt"
    outs = [
        ((n, D_CONV), F32),
        ((n, D_ATT), BF16),
        STACK,
        ((n, D_ATT), BF16),
        STACK,
        ((n, D_ATT), BF16),
        VT,
        ((n, D_RET), BF16),
        ((n, D_RET), BF16),
        ((n, D_RET), BF16),
        ((n, D_RET), BF16),
        ((n, D_RET), F32),
    ]
    out_shape, out_specs = [], []
    for o in outs:
        if o is VT:
            out_shape.append(jax.ShapeDtypeStruct((bsz, N_DIFF, nt, VT_ROWS, tm), BF16))
            out_specs.append(pl.BlockSpec((None, N_DIFF, None, VT_ROWS, tm), lambda b, i: (b, 0, i, 0, 0)))
        elif o is STACK:
            out_shape.append(jax.ShapeDtypeStruct((depth, n, N_DIFF, DIFF_VD), F32))
            out_specs.append(pl.BlockSpec((None, tm, N_DIFF, DIFF_VD), lambda b, i: (layer, b * nt + i, 0, 0)))
        else:
            out_shape.append(jax.ShapeDtypeStruct(*o))
            out_specs.append(pl.BlockSpec((tm, o[0][1]), row))
    in_specs = [
        pl.BlockSpec((tm, d), row),
        pl.BlockSpec((None, d, D_IN), lambda b, i: (layer, 0, 0), pipeline_mode=pl.Buffered(1)),
        pl.BlockSpec((tm, RET_HD), lambda b, i: (i, 0)),
        pl.BlockSpec((tm, RET_HD), lambda b, i: (i, 0)),
        pl.BlockSpec((tm, D_RET), lambda b, i: (0, 0)),
    ]
    operands = [h, w_in, cos, sin, wk_tab]
    aliases = {}
    for stack, out_idx in zip(kv_stacks, [k for k, o in enumerate(outs) if o is STACK]):
        aliases[len(operands)] = out_idx
        in_specs.append(pl.BlockSpec(memory_space=pl.ANY))
        operands.append(stack)
    return pl.pallas_call(
        functools.partial(_in_proj_body, tm, len(aliases)),
        grid=(bsz, nt),
        in_specs=in_specs,
        out_specs=out_specs,
        out_shape=out_shape,
        input_output_aliases=aliases,
        compiler_params=_cparams(("parallel", "parallel"), V7X_VMEM_LIMIT),
        name="in_proj",
    )(*operands)


def _rel_bucket(rel):
    half = N_BUCKETS // 2
    max_exact = half // 2
    ret = jnp.where(rel > 0, half, 0)
    n = jnp.abs(rel)
    nf = jnp.maximum(n, 1).astype(F32)
    large = max_exact + (jnp.log(nf / max_exact) / math.log(MAX_DISTANCE / max_exact)
                         * (half - max_exact)).astype(jnp.int32)
    large = jnp.minimum(large, half - 1)
    return ret + jnp.where(n < max_exact, n, large)


def _toeplitz_bias(rel_bias, rel_of_diff, rows, cols):
    period = rows + cols
    x = np.arange(period)
    diff = np.where(x < cols, x, x - period)
    rel = jnp.asarray(rel_of_diff(diff), jnp.int32)
    wp = rel_bias[_rel_bucket(rel)].astype(F32).T
    flat = jnp.tile(wp, (1, rows))[:, :rows * (period - 1)]
    return flat.reshape(-1, rows, period - 1)[:, :, :cols]


def _lam_value(lam_ref, lam_init):
    lf = lam_ref[...]
    a = jnp.sum(lf[0:1, :] * lf[1:2, :], axis=1, keepdims=True)
    b = jnp.sum(lf[2:3, :] * lf[3:4, :], axis=1, keepdims=True)
    return jnp.exp(a) - jnp.exp(b) + lam_init


def _block_diag_q(q):
    lane = lax.broadcasted_iota(jnp.int32, q.shape, 1)
    zero = jnp.zeros_like(q)
    return jnp.concatenate([jnp.where(lane < DIFF_HD, q, zero), jnp.where(lane >= DIFF_HD, q, zero)], axis=0)


def _attn_prompt_body(t, tv, nq, lam_init, q_ref, k_ref, vt_ref, bias_ref, lam_ref, subg_ref, o_ref,
                      qbd_ref, pa_ref, pb_ref, r_ref, g_ref, acc_ref):
    i = pl.program_id(2)
    sub = t // tv
    rb = ATTN_ROW_BLOCK
    lanes = 2 * t
    q_t = q_ref[...].astype(F32).T.astype(BF16)
    head_row = lax.broadcasted_iota(jnp.int32, q_t.shape, 0)
    zero = jnp.zeros_like(q_t)
    qbd_ref[:, 0:t] = jnp.where(head_row < DIFF_HD, q_t, zero)
    qbd_ref[:, t:lanes] = jnp.where(head_row >= DIFF_HD, q_t, zero)

    def tile_scores(j, bias_idx):
        k = k_ref[pl.ds(pl.multiple_of(j * t, t), t), :]
        s = _dot(k, qbd_ref[...])
        if bias_idx is not None:
            bias = bias_ref[bias_idx]
            s = s + jnp.concatenate([bias, bias], axis=1)
        return s

    def column_max(s):
        return jnp.max(s.reshape(t // 8, 8, lanes), axis=0)

    def step(j, bias_idx, p_ref, s=None):
        if s is None:
            s = tile_scores(j, bias_idx)
        g_ref[...] = jnp.maximum(g_ref[...], column_max(s))
        r = r_ref[...]
        for c in range(t // rb):
            p_ref[c * rb:(c + 1) * rb, :] = jnp.exp2((s[c * rb:(c + 1) * rb, :] - r).astype(BF16))
        pv = None
        for u in range(sub):
            part = _dot(vt_ref[j * sub + u], p_ref[u * tv:(u + 1) * tv, :])
            pv = part if pv is None else pv + part
        acc_ref[...] += pv

    def sweep(reference_from_diagonal):
        acc_ref[...] = jnp.zeros_like(acc_ref)
        g_ref[...] = jnp.full(g_ref.shape, MASKED, F32)

        def diagonal_scores():
            s = tile_scores(i, 0)
            if reference_from_diagonal:
                r_ref[...] = jnp.max(column_max(s), axis=0, keepdims=True)
            return s

        @pl.when(i == 0)
        def _():
            step(i, 0, pa_ref, diagonal_scores())

        @pl.when(i >= 1)
        def _():
            s_b = tile_scores(i - 1, 1)
            s_a = diagonal_scores()
            step(i, 0, pa_ref, s_a)
            step(i - 1, 1, pb_ref, s_b)

        n_far = jnp.maximum(i - 1, 0)
        rem = n_far % 4

        @pl.when(rem % 2 == 1)
        def _():
            step(i - 2, None, pa_ref)

        @pl.when(rem >= 2)
        def _():
            j = i - 2 - rem % 2
            s_b = tile_scores(j - 1, None)
            step(j, None, pa_ref)
            step(j - 1, None, pb_ref, s_b)

        def far_quad(c, carry):
            j = i - 2 - rem - 4 * c
            s_a = tile_scores(j, None)
            s_b = tile_scores(j - 1, None)
            s_c = tile_scores(j - 2, None)
            step(j, None, pa_ref, s_a)
            s_d = tile_scores(j - 3, None)
            step(j - 1, None, pb_ref, s_b)
            step(j - 2, None, pa_ref, s_c)
            step(j - 3, None, pb_ref, s_d)
            return carry

        lax.fori_loop(0, n_far // 4, far_quad, 0)

    sweep(True)
    true_max = jnp.max(g_ref[...], axis=0, keepdims=True)

    @pl.when(jnp.max(true_max - r_ref[...]) > REF_SLACK)
    def _():
        r_ref[...] = true_max
        sweep(False)

    acc = acc_ref[...]
    o = acc[0:DIFF_VD, :] / acc[DIFF_VD:DIFF_VD + 1, :]
    lam = _lam_value(lam_ref, lam_init)
    a = o[:, :t] - lam * o[:, t:]
    a = a * lax.rsqrt(jnp.mean(a * a, axis=0, keepdims=True) + LN_EPS)
    o_ref[...] = (a.T * subg_ref[...] * (1.0 - lam_init)).astype(o_ref.dtype)


def _attn_prompt(q, kb, vt, rel_bias, diff_lam, sub_g, layer, lam_init, bsz, seq, t):
    tv = vt.shape[-1]
    assert t >= MAX_DISTANCE and t % CHUNK == 0 and seq % t == 0 and t % tv == 0
    n = q.shape[0]
    nq = seq // t
    far_bias = rel_bias[_rel_bucket(jnp.full((), -MAX_DISTANCE, jnp.int32))].astype(F32)[:, None, None]
    b0 = _toeplitz_bias(rel_bias, lambda d: -d, t, t) - far_bias
    b1 = _toeplitz_bias(rel_bias, lambda d: -d - t, t, t) - far_bias
    r = jnp.arange(t, dtype=jnp.int32)[:, None]
    c = jnp.arange(t, dtype=jnp.int32)[None, :]
    b0 = jnp.where((r // CHUNK) <= (c // CHUNK), b0 * LOG2E, MASKED)
    tiles = jnp.stack([b0, b1 * LOG2E], axis=1)

    return pl.pallas_call(
        functools.partial(_attn_prompt_body, t, tv, nq, lam_init),
        grid=(bsz, N_DIFF, nq),
        in_specs=[
            pl.BlockSpec((t, DIFF_VD), lambda b, h, i: (b * nq + i, h)),
            pl.BlockSpec((seq, DIFF_VD), lambda b, h, i: (b, h)),
            pl.BlockSpec((None, None, seq // tv, VT_ROWS, tv), lambda b, h, i: (b, h, 0, 0, 0)),
            pl.BlockSpec((None, 2, t, t), lambda b, h, i: (h, 0, 0, 0), pipeline_mode=pl.Buffered(1)),
            pl.BlockSpec((None, 4, DIFF_HD), lambda b, h, i: (layer, 0, 0)),
            pl.BlockSpec((None, 1, DIFF_VD), lambda b, h, i: (layer, 0, 0)),
        ],
        out_specs=pl.BlockSpec((t, DIFF_VD), lambda b, h, i: (b * nq + i, h)),
        out_shape=jax.ShapeDtypeStruct((n, D_ATT), BF16),
        scratch_shapes=[
            pltpu.VMEM((DIFF_VD, 2 * t), BF16),
            pltpu.VMEM((t, 2 * t), BF16),
            pltpu.VMEM((t, 2 * t), BF16),
            pltpu.VMEM((1, 2 * t), F32),
            pltpu.VMEM((8, 2 * t), F32),
            pltpu.VMEM((VT_ROWS, 2 * t), F32),
        ],
        compiler_params=_cparams(("parallel", "parallel", "arbitrary"), V7X_VMEM_LIMIT),
        name="attn_prompt",
    )(q, kb, vt, tiles, diff_lam, sub_g.reshape(-1, 1, DIFF_VD))


def _attn_sample_body(tq, lam_init, q_ref, kp_ref, kn_ref, vp_ref, vn_ref, bp_ref, bn_ref, lam_ref, subg_ref, o_ref):
    qbd = _block_diag_q(q_ref[...])
    s_p = _dot_nt(qbd, kp_ref[...].astype(BF16)) + bp_ref[...]
    s_n = _dot_nt(qbd, kn_ref[...]) + bn_ref[...]
    m = jnp.maximum(jnp.max(s_p, axis=1, keepdims=True), jnp.max(s_n, axis=1, keepdims=True))
    p_p = jnp.exp2(s_p - m)
    p_n = jnp.exp2(s_n - m)
    l = jnp.sum(p_p, axis=1, keepdims=True) + jnp.sum(p_n, axis=1, keepdims=True)
    o = (_dot(p_p.astype(BF16), vp_ref[...].astype(BF16)) + _dot(p_n.astype(BF16), vn_ref[...])) / l
    lam = _lam_value(lam_ref, lam_init)
    a = o[:tq, :] - lam * o[tq:, :]
    a = a * lax.rsqrt(jnp.mean(a * a, axis=1, keepdims=True) + LN_EPS)
    o_ref[...] = (a * subg_ref[...] * (1.0 - lam_init)).astype(o_ref.dtype)


def _attn_sample(q, kb, vb, cache_k, cache_v, rel_bias, diff_lam, sub_g, layer, lam_init, bsz, tq):
    assert tq <= 128
    n = q.shape[0]
    past = cache_k.shape[2]
    bias = _toeplitz_bias(rel_bias, lambda d: d - past, tq, past + tq) * LOG2E
    qpos = past + jnp.arange(tq, dtype=jnp.int32)[:, None]
    kpos = jnp.arange(past + tq, dtype=jnp.int32)[None, :]
    bias = jnp.where((kpos // CHUNK) <= (qpos // CHUNK), bias, MASKED)
    bias = jnp.concatenate([bias, bias], axis=1)
    cache_k = cache_k.reshape(cache_k.shape[:3] + (D_ATT,))
    cache_v = cache_v.reshape(cache_v.shape[:3] + (D_ATT,))
    cache_spec = pl.BlockSpec((None, None, past, DIFF_VD), lambda b, h: (layer, b, 0, h))
    new_spec = pl.BlockSpec((tq, DIFF_VD), lambda b, h: (b, h))
    return pl.pallas_call(
        functools.partial(_attn_sample_body, tq, lam_init),
        grid=(bsz, N_DIFF),
        in_specs=[
            new_spec, cache_spec, new_spec, cache_spec, new_spec,
            pl.BlockSpec((None, 2 * tq, past), lambda b, h: (h, 0, 0)),
            pl.BlockSpec((None, 2 * tq, tq), lambda b, h: (h, 0, 0)),
            pl.BlockSpec((None, 4, DIFF_HD), lambda b, h: (layer, 0, 0)),
            pl.BlockSpec((None, 1, DIFF_VD), lambda b, h: (layer, 0, 0)),
        ],
        out_specs=new_spec,
        out_shape=jax.ShapeDtypeStruct((n, D_ATT), BF16),
        compiler_params=_cparams(("parallel", "parallel")),
        name="attn_sample",
    )(q, cache_k, kb, cache_v, vb, bias[:, :, :past], bias[:, :, past:], diff_lam, sub_g.reshape(-1, 1, DIFF_VD))


def _conv_retention_body(tm, nt, u_ref, prev_ref, cw_ref, cb_ref, cg_ref, cbeta_ref,
                         q_ref, k_ref, kw_ref, v_ref, gate_ref, s0_ref, decay_ref, wq_ref, gl_ref,
                         conv_ref, convnew_ref, ret_ref, snew_ref, ext_ref, s_ref):
    i = pl.program_id(1)
    hist = CONV_WIDTH - 1

    @pl.when(i == 0)
    def _():
        ext_ref[0, 0:CONV_PAD - hist, :] = jnp.zeros((CONV_PAD - hist, D_CONV), F32)
        ext_ref[0, CONV_PAD - hist:CONV_PAD, :] = prev_ref[...]
        s_ref[...] = s0_ref[...]

    @pl.when(i > 0)
    def _():
        ext_ref[0, 0:CONV_PAD, :] = ext_ref[0, tm:tm + CONV_PAD, :]

    ext_ref[0, CONV_PAD:CONV_PAD + tm, :] = u_ref[...]
    shifted_rows = tm + CONV_PAD - SUBLANES
    for s in range(1, SUBLANES):
        ext_ref[s, 0:shifted_rows, :] = ext_ref[0, s:s + shifted_rows, :]
    rb = 32
    cb = cb_ref[...]
    cg = cg_ref[...]
    cbeta = cbeta_ref[...]
    for r0 in range(0, tm, rb):
        acc = jnp.zeros((rb, D_CONV), F32)
        for w in range(CONV_WIDTH):
            shift = (CONV_PAD - hist + w) % SUBLANES
            start = CONV_PAD - hist + w - shift + r0
            acc = acc + ext_ref[shift, start:start + rb, :] * cw_ref[w:w + 1, :]
        y = _layer_norm_rows(acc + cb, cg, cbeta)
        conv_ref[r0:r0 + rb, :] = (y * _sigmoid(y)).astype(conv_ref.dtype)

    for h in range(N_RET):
        cols = slice(h * RET_HD, (h + 1) * RET_HD)
        q = q_ref[:, cols]
        v = v_ref[:, cols]
        s_prev = s_ref[h]
        inner = _dot_nt(q, k_ref[:, cols]) * decay_ref[h]
        o = _dot(inner.astype(BF16), v) + _dot(q, s_prev.astype(BF16)) * wq_ref[:, cols]
        s_ref[h] = gl_ref[h] * s_prev + _dot_tn(kw_ref[:, cols], v)
        mu = jnp.mean(o, axis=1, keepdims=True)
        d = o - mu
        var = jnp.mean(d * d, axis=1, keepdims=True)
        ret_ref[:, cols] = (gate_ref[:, cols] * (d * lax.rsqrt(var + LN_EPS))).astype(ret_ref.dtype)

    @pl.when(i == nt - 1)
    def _():
        convnew_ref[...] = ext_ref[0, CONV_PAD + tm - hist:CONV_PAD + tm, :]
        snew_ref[...] = s_ref[...]


def _retention_tables(lc):
    log_g = jnp.log1p(-jnp.exp2(-5.0 - jnp.arange(N_RET, dtype=F32)))
    idx = jnp.arange(lc, dtype=F32)
    rel = idx[:, None] - idx[None, :]
    decay = jnp.where(rel >= 0, jnp.exp(jnp.maximum(rel, 0.0)[None] * log_g[:, None, None]), 0.0)
    w_k = jnp.exp((lc - 1.0 - idx)[None, :] * log_g[:, None])
    w_q = jnp.exp((idx + 1.0)[None, :] * log_g[:, None])
    g_l = jnp.exp(lc * log_g)
    widen = lambda w: jnp.repeat(w.T, RET_HD, axis=1)
    gl_tab = jnp.broadcast_to(g_l[:, None, None], (N_RET, 1, RET_HD))
    return decay, widen(w_q), widen(w_k), gl_tab


def _conv_retention(u, conv_prev, conv_w, conv_b, conv_g, conv_beta, rq, rk, rkw, rv, gate, s0, decay, wq_tab, gl_tab,
                    layer, bsz, t, tm):
    n = u.shape[0]
    nt = t // tm
    assert t >= CONV_PAD and tm >= CONV_PAD and tm % 32 == 0
    row = lambda b, i: (b * nt + i, 0)
    tile = pl.BlockSpec((tm, D_RET), row)
    conv_tile = pl.BlockSpec((tm, D_CONV), row)
    conv_hist = pl.BlockSpec((None, CONV_WIDTH - 1, D_CONV), lambda b, i: (b, 0, 0))
    conv_vec = pl.BlockSpec((None, 1, D_CONV), lambda b, i: (layer, 0, 0))
    state = pl.BlockSpec((None, N_RET, RET_HD, RET_HD), lambda b, i: (b, 0, 0, 0))
    return pl.pallas_call(
        functools.partial(_conv_retention_body, tm, nt),
        grid=(bsz, nt),
        in_specs=[
            conv_tile, conv_hist,
            pl.BlockSpec((None, CONV_WIDTH, D_CONV), lambda b, i: (layer, 0, 0)),
            conv_vec, conv_vec, conv_vec,
            tile, tile, tile, tile, tile, state,
            pl.BlockSpec((N_RET, tm, tm), lambda b, i: (0, 0, 0)),
            pl.BlockSpec((tm, D_RET), lambda b, i: (0, 0)),
            pl.BlockSpec((N_RET, 1, RET_HD), lambda b, i: (0, 0, 0)),
        ],
        out_specs=[conv_tile, conv_hist, tile, state],
        out_shape=[
            jax.ShapeDtypeStruct((n, D_CONV), BF16),
            jax.ShapeDtypeStruct((bsz, CONV_WIDTH - 1, D_CONV), F32),
            jax.ShapeDtypeStruct((n, D_RET), BF16),
            jax.ShapeDtypeStruct((bsz, N_RET, RET_HD, RET_HD), F32),
        ],
        scratch_shapes=[pltpu.VMEM((SUBLANES, tm + CONV_PAD, D_CONV), F32), pltpu.VMEM((N_RET, RET_HD, RET_HD), F32)],
        compiler_params=_cparams(("parallel", "arbitrary")),
        name="conv_retention",
    )(u, conv_prev, conv_w, conv_b.reshape(-1, 1, D_CONV), conv_g.reshape(-1, 1, D_CONV),
      conv_beta.reshape(-1, 1, D_CONV), rq, rk, rkw, rv, gate, s0, decay, wq_tab, gl_tab)


def _out_proj_ln_body(alpha, x_ref, c_ref, a_ref, r_ref, w_ref, g_ref, b_ref, o_ref):
    e0 = D_CONV
    e1 = D_CONV + D_ATT
    mix = (_dot(c_ref[...], w_ref[0:e0, :]) + _dot(a_ref[...], w_ref[e0:e1, :])
           + _dot(r_ref[...], w_ref[e1:e1 + D_RET, :]))
    o_ref[...] = _layer_norm_rows(alpha * x_ref[...] + mix, g_ref[...], b_ref[...])


def _out_proj_ln(x, conv_out, attn_out, ret_out, w_out, g, b, layer, alpha, tm):
    n, d = x.shape
    d_mix = D_CONV + D_ATT + D_RET
    tile = lambda w: pl.BlockSpec((tm, w), lambda i: (i, 0))
    return pl.pallas_call(
        functools.partial(_out_proj_ln_body, alpha),
        grid=(n // tm,),
        in_specs=[
            tile(d), tile(D_CONV), tile(D_ATT), tile(D_RET),
            pl.BlockSpec((None, d_mix, d), lambda i: (layer, 0, 0), pipeline_mode=pl.Buffered(1)),
            pl.BlockSpec((1, d), lambda i: (0, 0)),
            pl.BlockSpec((1, d), lambda i: (0, 0)),
        ],
        out_specs=tile(d),
        out_shape=jax.ShapeDtypeStruct((n, d), F32),
        compiler_params=_cparams(("parallel",), V7X_VMEM_LIMIT),
        name="out_proj_ln",
    )(x, conv_out, attn_out, ret_out, w_out, g, b)


def _rope_tables(pos0, t):
    half = RET_HD // 2
    inv = 1.0 / (ROPE_BASE ** (jnp.arange(half, dtype=F32) / half))
    ang = (pos0 + jnp.arange(t, dtype=jnp.int32)).astype(F32)[:, None] * inv[None, :]
    cos = jnp.cos(ang)
    sin = jnp.sin(ang)
    return jnp.concatenate([cos, cos], axis=1), jnp.concatenate([-sin, sin], axis=1)


def _trunk(x, pos0, caches, params, tiles):
    (ln_g, ln_b, wg, wu, wd, w_in, w_out, conv_w, conv_b, conv_ln_g, conv_ln_b, diff_lam, diff_sub_g,
     rel_bias) = params
    bsz, t, d = x.shape
    depth = ln_g.shape[0]
    alpha = (2 * depth) ** 0.25
    tm_ffn, tf, tm_tok, tm_out, t_att = tiles
    prompt = caches is None
    lc = tm_tok
    cos, sin = _rope_tables(pos0, t)
    decay, wq_tab, wk_tab, gl_tab = _retention_tables(lc)
    x = x.reshape(bsz * t, d)
    conv_l, s_l = [], []
    kv_stacks = tuple(jnp.zeros((depth, bsz * t, N_DIFF, DIFF_VD), F32) for _ in range(2))
    for l in range(depth):
        lam_init = 0.8 - 0.6 * math.exp(-0.3 * l)
        if prompt:
            conv_prev = jnp.zeros((bsz, CONV_WIDTH - 1, D_CONV), F32)
            ret_prev = jnp.zeros((bsz, N_RET, RET_HD, RET_HD), F32)
        else:
            cache_conv, cache_k, cache_v, state_ret = caches
            conv_prev, ret_prev = cache_conv[l], state_ret[l]
        x = _ffn_ln(x, wg, wu, wd, ln_g[l, 0:1], ln_b[l, 0:1], l, 0, alpha, tm_ffn, tf)
        u, q, k_stack, kb, v_stack, vb, vt, rq, rk, rkw, rv, gate = _in_proj(
            x, w_in, l, depth, kv_stacks, cos, sin, wk_tab, bsz, t, tm_tok)
        kv_stacks = (k_stack, v_stack)
        if prompt:
            attn_out = _attn_prompt(q, kb, vt, rel_bias, diff_lam, diff_sub_g, l, lam_init, bsz, t, t_att)
        else:
            attn_out = _attn_sample(q, kb, vb, cache_k, cache_v, rel_bias, diff_lam, diff_sub_g, l, lam_init, bsz, t)
        conv_out, conv_new, ret_out, s_new = _conv_retention(
            u, conv_prev, conv_w, conv_b, conv_ln_g, conv_ln_b, rq, rk, rkw, rv, gate, ret_prev, decay, wq_tab, gl_tab,
            l, bsz, t, tm_tok)
        x = _out_proj_ln(x, conv_out, attn_out, ret_out, w_out, ln_g[l, 1:2], ln_b[l, 1:2], l, alpha, tm_out)
        x = _ffn_ln(x, wg, wu, wd, ln_g[l, 2:3], ln_b[l, 2:3], l, 1, alpha, tm_ffn, tf)
        conv_l.append(conv_new)
        s_l.append(s_new)
    k_new, v_new = (a.reshape(depth, bsz, t, N_DIFF, DIFF_VD) for a in kv_stacks)
    return x.reshape(bsz, t, d), jnp.stack(conv_l), k_new, v_new, jnp.stack(s_l)


def _pick(n, cap):
    if n <= cap:
        return n
    best = None
    for c in range(128, cap + 1, 128):
        if n % c == 0:
            best = c
    assert best is not None
    return best


def kernel(x_prompt, x_sample, cache_conv, cache_k, cache_v, state_ret, ln_g, ln_b, ffn_w_gate, ffn_w_up,
           ffn_w_down, w_in, w_out, conv_w, conv_b, conv_ln_g, conv_ln_b, diff_lam, diff_sub_g, rel_bias):
    tf = _pick(ffn_w_gate.shape[-1], 512)
    params = (ln_g, ln_b, ffn_w_gate.astype(BF16), ffn_w_up.astype(BF16), ffn_w_down.astype(BF16),
              w_in.astype(BF16), w_out.astype(BF16), conv_w, conv_b, conv_ln_g, conv_ln_b, diff_lam, diff_sub_g,
              rel_bias)
    bp, tp, _ = x_prompt.shape
    bs, ts, _ = x_sample.shape
    tok_p = _pick(tp, 256)
    y_p, conv_p, k_p, v_p, ret_p = _trunk(
        x_prompt, 0, None, params, (_pick(bp * tp, 512), tf, tok_p, _pick(bp * tp, 512), _pick(tp, 512)))
    y_s, conv_s, k_s, v_s, ret_s = _trunk(
        x_sample, cache_k.shape[2], (cache_conv, cache_k, cache_v, state_ret), params,
        (_pick(bs * ts, 512), tf, ts, _pick(bs * ts, 512), None))
    return (y_p, y_s, conv_p, k_p, v_p, ret_p, conv_s, k_s, v_s, ret_s)
```

```python
import functools
import math

import jax
import jax.numpy as jnp
import numpy as np
from jax import lax
from jax.experimental import pallas as pl
from jax.experimental.pallas import tpu as pltpu

CHUNK = 64
CONV_WIDTH = 31
D_CONV = 512
N_DIFF = 8
DIFF_HD = 64
DIFF_VD = 2 * DIFF_HD
N_RET = 4
RET_HD = 128
N_BUCKETS = 32
MAX_DISTANCE = 128
LN_EPS = 1e-5
ROPE_BASE = 10000.0

D_ATT = N_DIFF * DIFF_VD
VT_ROWS = DIFF_VD + 16
D_RET = N_RET * RET_HD
C_CA, C_CG = 0, D_CONV
C_Q = 2 * D_CONV
C_K = C_Q + D_ATT
C_V = C_K + D_ATT
C_RQ = C_V + D_ATT
C_RK = C_RQ + D_RET
C_RV = C_RK + D_RET
C_RG = C_RV + D_RET
D_IN = C_RG + D_RET

LOG2E = math.log2(math.e)
REF_SLACK = 8.0
MASKED = -1e30
ATTN_ROW_BLOCK = 32
SUBLANES = 8
CONV_PAD = 32
V7X_VMEM_LIMIT = 56 * 1024 * 1024

F32 = jnp.float32
BF16 = jnp.bfloat16


def _cparams(sem, vmem=None):
    return pltpu.CompilerParams(dimension_semantics=sem, vmem_limit_bytes=vmem)


def _sigmoid(x):
    return 1.0 / (1.0 + jnp.exp(-x))


def _layer_norm_rows(y, g, b):
    mu = jnp.mean(y, axis=-1, keepdims=True)
    d = y - mu
    var = jnp.mean(d * d, axis=-1, keepdims=True)
    return d * lax.rsqrt(var + LN_EPS) * g + b


def _dot(a, b):
    return jnp.dot(a, b, preferred_element_type=F32)


def _dot_nt(a, b):
    return lax.dot_general(a, b, (((1,), (1,)), ((), ())), preferred_element_type=F32)


def _dot_tn(a, b):
    return lax.dot_general(a, b, (((0,), (0,)), ((), ())), preferred_element_type=F32)


def _ffn_ln_body(alpha, nf, x_ref, wg_ref, wu_ref, wd_ref, g_ref, b_ref, o_ref, xb_ref):
    f = pl.program_id(1)

    @pl.when(f == 0)
    def _():
        xb_ref[...] = x_ref[...].astype(BF16)
        o_ref[...] = jnp.zeros_like(o_ref)

    xb = xb_ref[...]
    hg = _dot(xb, wg_ref[...])
    hu = _dot(xb, wu_ref[...])
    act = (hg * _sigmoid(hg) * hu).astype(BF16)
    o_ref[...] += _dot(act, wd_ref[...])

    @pl.when(f == nf - 1)
    def _():
        y = alpha * x_ref[...] + 0.5 * o_ref[...]
        o_ref[...] = _layer_norm_rows(y, g_ref[...], b_ref[...])


def _ffn_ln(x, wg, wu, wd, g, b, layer, slot, alpha, tm, tf):
    n, d = x.shape
    nf = wg.shape[-1] // tf
    return pl.pallas_call(
        functools.partial(_ffn_ln_body, alpha, nf),
        grid=(n // tm, nf),
        in_specs=[
            pl.BlockSpec((tm, d), lambda i, f: (i, 0)),
            pl.BlockSpec((None, None, d, tf), lambda i, f: (layer, slot, 0, f)),
            pl.BlockSpec((None, None, d, tf), lambda i, f: (layer, slot, 0, f)),
            pl.BlockSpec((None, None, tf, d), lambda i, f: (layer, slot, f, 0)),
            pl.BlockSpec((1, d), lambda i, f: (0, 0)),
            pl.BlockSpec((1, d), lambda i, f: (0, 0)),
        ],
        out_specs=pl.BlockSpec((tm, d), lambda i, f: (i, 0)),
        out_shape=jax.ShapeDtypeStruct((n, d), F32),
        scratch_shapes=[pltpu.VMEM((tm, d), BF16)],
        compiler_params=_cparams(("parallel", "arbitrary"), V7X_VMEM_LIMIT),
        name="ffn_ln",
    )(x, wg, wu, wd, g, b)


def _rope_heads(x, cos, sin):
    outs = []
    for h in range(N_RET):
        xh = x[:, h * RET_HD:(h + 1) * RET_HD]
        outs.append(xh * cos + pltpu.roll(xh, RET_HD // 2, 1) * sin)
    return jnp.concatenate(outs, axis=1)


def _in_proj_body(tm, n_aliased, h_ref, w_ref, cos_ref, sin_ref, wk_ref, *refs):
    u_ref, q_ref, kf_ref, kb_ref, vf_ref, vb_ref, vt_ref, rq_ref, rk_ref, rkw_ref, rv_ref, rg_ref = refs[n_aliased:]
    hb = h_ref[...].astype(BF16)

    def mm(lo, hi):
        return _dot(hb, w_ref[:, lo:hi])

    u_ref[...] = mm(C_CA, C_CG) * _sigmoid(mm(C_CG, C_Q))
    q_ref[...] = (mm(C_Q, C_K) * (DIFF_HD ** -0.5 * LOG2E)).astype(BF16)
    k = mm(C_K, C_V)
    kf_ref[...] = k.reshape(tm, N_DIFF, DIFF_VD)
    kb_ref[...] = k.astype(BF16)
    v = mm(C_V, C_RQ)
    vf_ref[...] = v.reshape(tm, N_DIFF, DIFF_VD)
    vb_ref[...] = v.astype(BF16)
    vt_ref[:, 0:DIFF_VD, :] = v.T.reshape(N_DIFF, DIFF_VD, tm).astype(BF16)
    pad_row = lax.broadcasted_iota(jnp.int32, (N_DIFF, VT_ROWS - DIFF_VD, tm), 1)
    vt_ref[:, DIFF_VD:VT_ROWS, :] = jnp.where(pad_row == 0, 1.0, 0.0).astype(BF16)
    cos = cos_ref[...]
    sin = sin_ref[...]
    rq_ref[...] = _rope_heads(mm(C_RQ, C_RK), cos, sin).astype(BF16)
    rk = _rope_heads(mm(C_RK, C_RV), cos, sin) * (RET_HD ** -0.5)
    rk_ref[...] = rk.astype(BF16)
    rkw_ref[...] = (rk * wk_ref[...]).astype(BF16)
    rv_ref[...] = mm(C_RV, C_RG).astype(BF16)
    rg = mm(C_RG, D_IN)
    rg_ref[...] = rg * _sigmoid(rg)


def _in_proj(h, w_in, layer, depth, kv_stacks, cos, sin, wk_tab, bsz, t, tm):
    n, d = h.shape
    nt = t // tm
    row = lambda b, i: (b * nt + i, 0)
    STACK, VT = "stack", "vt"
    outs = [
        ((n, D_CONV), F32),
        ((n, D_ATT), BF16),
        STACK,
        ((n, D_ATT), BF16),
        STACK,
        ((n, D_ATT), BF16),
        VT,
        ((n, D_RET), BF16),
        ((n, D_RET), BF16),
        ((n, D_RET), BF16),
        ((n, D_RET), BF16),
        ((n, D_RET), F32),
    ]
    out_shape, out_specs = [], []
    for o in outs:
        if o is VT:
            out_shape.append(jax.ShapeDtypeStruct((bsz, N_DIFF, nt, VT_ROWS, tm), BF16))
            out_specs.append(pl.BlockSpec((None, N_DIFF, None, VT_ROWS, tm), lambda b, i: (b, 0, i, 0, 0)))
        elif o is STACK:
            out_shape.append(jax.ShapeDtypeStruct((depth, n, N_DIFF, DIFF_VD), F32))
            out_specs.append(pl.BlockSpec((None, tm, N_DIFF, DIFF_VD), lambda b, i: (layer, b * nt + i, 0, 0)))
        else:
            out_shape.append(jax.ShapeDtypeStruct(*o))
            out_specs.append(pl.BlockSpec((tm, o[0][1]), row))
    in_specs = [
        pl.BlockSpec((tm, d), row),
        pl.BlockSpec((None, d, D_IN), lambda b, i: (layer, 0, 0), pipeline_mode=pl.Buffered(1)),
        pl.BlockSpec((tm, RET_HD), lambda b, i: (i, 0)),
        pl.BlockSpec((tm, RET_HD), lambda b, i: (i, 0)),
        pl.BlockSpec((tm, D_RET), lambda b, i: (0, 0)),
    ]
    operands = [h, w_in, cos, sin, wk_tab]
    aliases = {}
    for stack, out_idx in zip(kv_stacks, [k for k, o in enumerate(outs) if o is STACK]):
        aliases[len(operands)] = out_idx
        in_specs.append(pl.BlockSpec(memory_space=pl.ANY))
        operands.append(stack)
    return pl.pallas_call(
        functools.partial(_in_proj_body, tm, len(aliases)),
        grid=(bsz, nt),
        in_specs=in_specs,
        out_specs=out_specs,
        out_shape=out_shape,
        input_output_aliases=aliases,
        compiler_params=_cparams(("parallel", "parallel"), V7X_VMEM_LIMIT),
        name="in_proj",
    )(*operands)


def _rel_bucket(rel):
    half = N_BUCKETS // 2
    max_exact = half // 2
    ret = jnp.where(rel > 0, half, 0)
    n = jnp.abs(rel)
    nf = jnp.maximum(n, 1).astype(F32)
    large = max_exact + (jnp.log(nf / max_exact) / math.log(MAX_DISTANCE / max_exact)
                         * (half - max_exact)).astype(jnp.int32)
    large = jnp.minimum(large, half - 1)
    return ret + jnp.where(n < max_exact, n, large)


def _toeplitz_bias(rel_bias, rel_of_diff, rows, cols):
    period = rows + cols
    x = np.arange(period)
    diff = np.where(x < cols, x, x - period)
    rel = jnp.asarray(rel_of_diff(diff), jnp.int32)
    wp = rel_bias[_rel_bucket(rel)].astype(F32).T
    flat = jnp.tile(wp, (1, rows))[:, :rows * (period - 1)]
    return flat.reshape(-1, rows, period - 1)[:, :, :cols]


def _lam_value(lam_ref, lam_init):
    lf = lam_ref[...]
    a = jnp.sum(lf[0:1, :] * lf[1:2, :], axis=1, keepdims=True)
    b = jnp.sum(lf[2:3, :] * lf[3:4, :], axis=1, keepdims=True)
    return jnp.exp(a) - jnp.exp(b) + lam_init


def _block_diag_q(q):
    lane = lax.broadcasted_iota(jnp.int32, q.shape, 1)
    zero = jnp.zeros_like(q)
    return jnp.concatenate([jnp.where(lane < DIFF_HD, q, zero), jnp.where(lane >= DIFF_HD, q, zero)], axis=0)


def _attn_prompt_body(t, tv, nq, lam_init, q_ref, k_ref, vt_ref, bias_ref, lam_ref, subg_ref, o_ref,
                      qbd_ref, pa_ref, pb_ref, r_ref, g_ref, acc_ref):
    i = pl.program_id(2)
    sub = t // tv
    rb = ATTN_ROW_BLOCK
    lanes = 2 * t
    q_t = q_ref[...].astype(F32).T.astype(BF16)
    head_row = lax.broadcasted_iota(jnp.int32, q_t.shape, 0)
    zero = jnp.zeros_like(q_t)
    qbd_ref[:, 0:t] = jnp.where(head_row < DIFF_HD, q_t, zero)
    qbd_ref[:, t:lanes] = jnp.where(head_row >= DIFF_HD, q_t, zero)

    def tile_scores(j, bias_idx):
        k = k_ref[pl.ds(pl.multiple_of(j * t, t), t), :]
        s = _dot(k, qbd_ref[...])
        if bias_idx is not None:
            bias = bias_ref[bias_idx]
            s = s + jnp.concatenate([bias, bias], axis=1)
        return s

    def column_max(s):
        return jnp.max(s.reshape(t // 8, 8, lanes), axis=0)

    def step(j, bias_idx, p_ref, s=None):
        if s is None:
            s = tile_scores(j, bias_idx)
        g_ref[...] = jnp.maximum(g_ref[...], column_max(s))
        r = r_ref[...]
        for c in range(t // rb):
            p_ref[c * rb:(c + 1) * rb, :] = jnp.exp2((s[c * rb:(c + 1) * rb, :] - r).astype(BF16))
        pv = None
        for u in range(sub):
            part = _dot(vt_ref[j * sub + u], p_ref[u * tv:(u + 1) * tv, :])
            pv = part if pv is None else pv + part
        acc_ref[...] += pv

    def sweep(reference_from_diagonal):
        acc_ref[...] = jnp.zeros_like(acc_ref)
        g_ref[...] = jnp.full(g_ref.shape, MASKED, F32)

        def diagonal_scores():
            s = tile_scores(i, 0)
            if reference_from_diagonal:
                r_ref[...] = jnp.max(column_max(s), axis=0, keepdims=True)
            return s

        @pl.when(i == 0)
        def _():
            step(i, 0, pa_ref, diagonal_scores())

        @pl.when(i >= 1)
        def _():
            s_b = tile_scores(i - 1, 1)
            s_a = diagonal_scores()
            step(i, 0, pa_ref, s_a)
            step(i - 1, 1, pb_ref, s_b)

        n_far = jnp.maximum(i - 1, 0)
        rem = n_far % 4

        @pl.when(rem % 2 == 1)
        def _():
            step(i - 2, None, pa_ref)

        @pl.when(rem >= 2)
        def _():
            j = i - 2 - rem % 2
            s_b = tile_scores(j - 1, None)
            step(j, None, pa_ref)
            step(j - 1, None, pb_ref, s_b)

        def far_quad(c, carry):
            j = i - 2 - rem - 4 * c
            s_a = tile_scores(j, None)
            s_b = tile_scores(j - 1, None)
            s_c = tile_scores(j - 2, None)
            step(j, None, pa_ref, s_a)
            s_d = tile_scores(j - 3, None)
            step(j - 1, None, pb_ref, s_b)
            step(j - 2, None, pa_ref, s_c)
            step(j - 3, None, pb_ref, s_d)
            return carry

        lax.fori_loop(0, n_far // 4, far_quad, 0)

    sweep(True)
    true_max = jnp.max(g_ref[...], axis=0, keepdims=True)

    @pl.when(jnp.max(true_max - r_ref[...]) > REF_SLACK)
    def _():
        r_ref[...] = true_max
        sweep(False)

    acc = acc_ref[...]
    o = acc[0:DIFF_VD, :] / acc[DIFF_VD:DIFF_VD + 1, :]
    lam = _lam_value(lam_ref, lam_init)
    a = o[:, :t] - lam * o[:, t:]
    a = a * lax.rsqrt(jnp.mean(a * a, axis=0, keepdims=True) + LN_EPS)
    o_ref[...] = (a.T * subg_ref[...] * (1.0 - lam_init)).astype(o_ref.dtype)


def _attn_prompt(q, kb, vt, rel_bias, diff_lam, sub_g, layer, lam_init, bsz, seq, t):
    tv = vt.shape[-1]
    assert t >= MAX_DISTANCE and t % CHUNK == 0 and seq % t == 0 and t % tv == 0
    n = q.shape[0]
    nq = seq // t
    far_bias = rel_bias[_rel_bucket(jnp.full((), -MAX_DISTANCE, jnp.int32))].astype(F32)[:, None, None]
    b0 = _toeplitz_bias(rel_bias, lambda d: -d, t, t) - far_bias
    b1 = _toeplitz_bias(rel_bias, lambda d: -d - t, t, t) - far_bias
    r = jnp.arange(t, dtype=jnp.int32)[:, None]
    c = jnp.arange(t, dtype=jnp.int32)[None, :]
    b0 = jnp.where((r // CHUNK) <= (c // CHUNK), b0 * LOG2E, MASKED)
    tiles = jnp.stack([b0, b1 * LOG2E], axis=1)

    return pl.pallas_call(
        functools.partial(_attn_prompt_body, t, tv, nq, lam_init),
        grid=(bsz, N_DIFF, nq),
        in_specs=[
            pl.BlockSpec((t, DIFF_VD), lambda b, h, i: (b * nq + i, h)),
            pl.BlockSpec((seq, DIFF_VD), lambda b, h, i: (b, h)),
            pl.BlockSpec((None, None, seq // tv, VT_ROWS, tv), lambda b, h, i: (b, h, 0, 0, 0)),
            pl.BlockSpec((None, 2, t, t), lambda b, h, i: (h, 0, 0, 0), pipeline_mode=pl.Buffered(1)),
            pl.BlockSpec((None, 4, DIFF_HD), lambda b, h, i: (layer, 0, 0)),
            pl.BlockSpec((None, 1, DIFF_VD), lambda b, h, i: (layer, 0, 0)),
        ],
        out_specs=pl.BlockSpec((t, DIFF_VD), lambda b, h, i: (b * nq + i, h)),
        out_shape=jax.ShapeDtypeStruct((n, D_ATT), BF16),
        scratch_shapes=[
            pltpu.VMEM((DIFF_VD, 2 * t), BF16),
            pltpu.VMEM((t, 2 * t), BF16),
            pltpu.VMEM((t, 2 * t), BF16),
            pltpu.VMEM((1, 2 * t), F32),
            pltpu.VMEM((8, 2 * t), F32),
            pltpu.VMEM((VT_ROWS, 2 * t), F32),
        ],
        compiler_params=_cparams(("parallel", "parallel", "arbitrary"), V7X_VMEM_LIMIT),
        name="attn_prompt",
    )(q, kb, vt, tiles, diff_lam, sub_g.reshape(-1, 1, DIFF_VD))


def _attn_sample_body(tq, lam_init, q_ref, kp_ref, kn_ref, vp_ref, vn_ref, bp_ref, bn_ref, lam_ref, subg_ref, o_ref):
    lam = _lam_value(lam_ref, lam_init)
    for h in range(N_DIFF):
        cols = slice(h * DIFF_VD, (h + 1) * DIFF_VD)
        qbd = _block_diag_q(q_ref[:, cols])
        s_p = _dot_nt(qbd, kp_ref[:, cols].astype(BF16)) + bp_ref[h]
        s_n = _dot_nt(qbd, kn_ref[:, cols]) + bn_ref[h]
        m = jnp.maximum(jnp.max(s_p, axis=1, keepdims=True), jnp.max(s_n, axis=1, keepdims=True))
        p_p = jnp.exp2(s_p - m)
        p_n = jnp.exp2(s_n - m)
        l = jnp.sum(p_p, axis=1, keepdims=True) + jnp.sum(p_n, axis=1, keepdims=True)
        o = (_dot(p_p.astype(BF16), vp_ref[:, cols].astype(BF16)) + _dot(p_n.astype(BF16), vn_ref[:, cols])) / l
        a = o[:tq, :] - lam * o[tq:, :]
        a = a * lax.rsqrt(jnp.mean(a * a, axis=1, keepdims=True) + LN_EPS)
        o_ref[:, cols] = (a * subg_ref[...] * (1.0 - lam_init)).astype(o_ref.dtype)


def _attn_sample(q, kb, vb, cache_k, cache_v, rel_bias, diff_lam, sub_g, layer, lam_init, bsz, tq):
    assert tq <= 128
    n = q.shape[0]
    past = cache_k.shape[2]
    bias = _toeplitz_bias(rel_bias, lambda d: d - past, tq, past + tq) * LOG2E
    qpos = past + jnp.arange(tq, dtype=jnp.int32)[:, None]
    kpos = jnp.arange(past + tq, dtype=jnp.int32)[None, :]
    bias = jnp.where((kpos // CHUNK) <= (qpos // CHUNK), bias, MASKED)
    bias = jnp.concatenate([bias, bias], axis=1)
    cache_k = cache_k.reshape(cache_k.shape[:3] + (D_ATT,))
    cache_v = cache_v.reshape(cache_v.shape[:3] + (D_ATT,))
    cache_spec = pl.BlockSpec((None, None, past, D_ATT), lambda b: (layer, b, 0, 0))
    new_spec = pl.BlockSpec((tq, D_ATT), lambda b: (b, 0))
    return pl.pallas_call(
        functools.partial(_attn_sample_body, tq, lam_init),
        grid=(bsz,),
        in_specs=[
            new_spec, cache_spec, new_spec, cache_spec, new_spec,
            pl.BlockSpec((N_DIFF, 2 * tq, past), lambda b: (0, 0, 0), pipeline_mode=pl.Buffered(1)),
            pl.BlockSpec((N_DIFF, 2 * tq, tq), lambda b: (0, 0, 0), pipeline_mode=pl.Buffered(1)),
            pl.BlockSpec((None, 4, DIFF_HD), lambda b: (layer, 0, 0)),
            pl.BlockSpec((None, 1, DIFF_VD), lambda b: (layer, 0, 0)),
        ],
        out_specs=new_spec,
        out_shape=jax.ShapeDtypeStruct((n, D_ATT), BF16),
        compiler_params=_cparams(("parallel",), V7X_VMEM_LIMIT),
        name="attn_sample",
    )(q, cache_k, kb, cache_v, vb, bias[:, :, :past], bias[:, :, past:], diff_lam, sub_g.reshape(-1, 1, DIFF_VD))


def _conv_retention_body(tm, nt, u_ref, prev_ref, cw_ref, cb_ref, cg_ref, cbeta_ref,
                         q_ref, k_ref, kw_ref, v_ref, gate_ref, s0_ref, decay_ref, wq_ref, gl_ref,
                         conv_ref, convnew_ref, ret_ref, snew_ref, ext_ref, s_ref):
    i = pl.program_id(1)
    hist = CONV_WIDTH - 1

    @pl.when(i == 0)
    def _():
        ext_ref[0, 0:CONV_PAD - hist, :] = jnp.zeros((CONV_PAD - hist, D_CONV), F32)
        ext_ref[0, CONV_PAD - hist:CONV_PAD, :] = prev_ref[...]
        s_ref[...] = s0_ref[...]

    @pl.when(i > 0)
    def _():
        ext_ref[0, 0:CONV_PAD, :] = ext_ref[0, tm:tm + CONV_PAD, :]

    ext_ref[0, CONV_PAD:CONV_PAD + tm, :] = u_ref[...]
    shifted_rows = tm + CONV_PAD - SUBLANES
    for s in range(1, SUBLANES):
        ext_ref[s, 0:shifted_rows, :] = ext_ref[0, s:s + shifted_rows, :]
    rb = 32
    cb = cb_ref[...]
    cg = cg_ref[...]
    cbeta = cbeta_ref[...]
    for r0 in range(0, tm, rb):
        acc = jnp.zeros((rb, D_CONV), F32)
        for w in range(CONV_WIDTH):
            shift = (CONV_PAD - hist + w) % SUBLANES
            start = CONV_PAD - hist + w - shift + r0
            acc = acc + ext_ref[shift, start:start + rb, :] * cw_ref[w:w + 1, :]
        y = _layer_norm_rows(acc + cb, cg, cbeta)
        conv_ref[r0:r0 + rb, :] = (y * _sigmoid(y)).astype(conv_ref.dtype)

    for h in range(N_RET):
        cols = slice(h * RET_HD, (h + 1) * RET_HD)
        q = q_ref[:, cols]
        v = v_ref[:, cols]
        s_prev = s_ref[h]
        inner = _dot_nt(q, k_ref[:, cols]) * decay_ref[h]
        o = _dot(inner.astype(BF16), v) + _dot(q, s_prev.astype(BF16)) * wq_ref[:, cols]
        s_ref[h] = gl_ref[h] * s_prev + _dot_tn(kw_ref[:, cols], v)
        mu = jnp.mean(o, axis=1, keepdims=True)
        d = o - mu
        var = jnp.mean(d * d, axis=1, keepdims=True)
        ret_ref[:, cols] = (gate_ref[:, cols] * (d * lax.rsqrt(var + LN_EPS))).astype(ret_ref.dtype)

    @pl.when(i == nt - 1)
    def _():
        convnew_ref[...] = ext_ref[0, CONV_PAD + tm - hist:CONV_PAD + tm, :]
        snew_ref[...] = s_ref[...]


def _retention_tables(lc):
    log_g = jnp.log1p(-jnp.exp2(-5.0 - jnp.arange(N_RET, dtype=F32)))
    idx = jnp.arange(lc, dtype=F32)
    rel = idx[:, None] - idx[None, :]
    decay = jnp.where(rel >= 0, jnp.exp(jnp.maximum(rel, 0.0)[None] * log_g[:, None, None]), 0.0)
    w_k = jnp.exp((lc - 1.0 - idx)[None, :] * log_g[:, None])
    w_q = jnp.exp((idx + 1.0)[None, :] * log_g[:, None])
    g_l = jnp.exp(lc * log_g)
    widen = lambda w: jnp.repeat(w.T, RET_HD, axis=1)
    gl_tab = jnp.broadcast_to(g_l[:, None, None], (N_RET, 1, RET_HD))
    return decay, widen(w_q), widen(w_k), gl_tab


def _conv_retention(u, conv_prev, conv_w, conv_b, conv_g, conv_beta, rq, rk, rkw, rv, gate, s0, decay, wq_tab, gl_tab,
                    layer, bsz, t, tm):
    n = u.shape[0]
    nt = t // tm
    assert t >= CONV_PAD and tm >= CONV_PAD and tm % 32 == 0
    row = lambda b, i: (b * nt + i, 0)
    tile = pl.BlockSpec((tm, D_RET), row)
    conv_tile = pl.BlockSpec((tm, D_CONV), row)
    conv_hist = pl.BlockSpec((None, CONV_WIDTH - 1, D_CONV), lambda b, i: (b, 0, 0))
    conv_vec = pl.BlockSpec((None, 1, D_CONV), lambda b, i: (layer, 0, 0))
    state = pl.BlockSpec((None, N_RET, RET_HD, RET_HD), lambda b, i: (b, 0, 0, 0))
    return pl.pallas_call(
        functools.partial(_conv_retention_body, tm, nt),
        grid=(bsz, nt),
        in_specs=[
            conv_tile, conv_hist,
            pl.BlockSpec((None, CONV_WIDTH, D_CONV), lambda b, i: (layer, 0, 0)),
            conv_vec, conv_vec, conv_vec,
            tile, tile, tile, tile, tile, state,
            pl.BlockSpec((N_RET, tm, tm), lambda b, i: (0, 0, 0)),
            pl.BlockSpec((tm, D_RET), lambda b, i: (0, 0)),
            pl.BlockSpec((N_RET, 1, RET_HD), lambda b, i: (0, 0, 0)),
        ],
        out_specs=[conv_tile, conv_hist, tile, state],
        out_shape=[
            jax.ShapeDtypeStruct((n, D_CONV), BF16),
            jax.ShapeDtypeStruct((bsz, CONV_WIDTH - 1, D_CONV), F32),
            jax.ShapeDtypeStruct((n, D_RET), BF16),
            jax.ShapeDtypeStruct((bsz, N_RET, RET_HD, RET_HD), F32),
        ],
        scratch_shapes=[pltpu.VMEM((SUBLANES, tm + CONV_PAD, D_CONV), F32), pltpu.VMEM((N_RET, RET_HD, RET_HD), F32)],
        compiler_params=_cparams(("parallel", "arbitrary")),
        name="conv_retention",
    )(u, conv_prev, conv_w, conv_b.reshape(-1, 1, D_CONV), conv_g.reshape(-1, 1, D_CONV),
      conv_beta.reshape(-1, 1, D_CONV), rq, rk, rkw, rv, gate, s0, decay, wq_tab, gl_tab)


def _out_proj_ln_body(alpha, x_ref, c_ref, a_ref, r_ref, w_ref, g_ref, b_ref, o_ref):
    e0 = D_CONV
    e1 = D_CONV + D_ATT
    mix = (_dot(c_ref[...], w_ref[0:e0, :]) + _dot(a_ref[...], w_ref[e0:e1, :])
           + _dot(r_ref[...], w_ref[e1:e1 + D_RET, :]))
    o_ref[...] = _layer_norm_rows(alpha * x_ref[...] + mix, g_ref[...], b_ref[...])


def _out_proj_ln(x, conv_out, attn_out, ret_out, w_out, g, b, layer, alpha, tm):
    n, d = x.shape
    d_mix = D_CONV + D_ATT + D_RET
    tile = lambda w: pl.BlockSpec((tm, w), lambda i: (i, 0))
    return pl.pallas_call(
        functools.partial(_out_proj_ln_body, alpha),
        grid=(n // tm,),
        in_specs=[
            tile(d), tile(D_CONV), tile(D_ATT), tile(D_RET),
            pl.BlockSpec((None, d_mix, d), lambda i: (layer, 0, 0), pipeline_mode=pl.Buffered(1)),
            pl.BlockSpec((1, d), lambda i: (0, 0)),
            pl.BlockSpec((1, d), lambda i: (0, 0)),
        ],
        out_specs=tile(d),
        out_shape=jax.ShapeDtypeStruct((n, d), F32),
        compiler_params=_cparams(("parallel",), V7X_VMEM_LIMIT),
        name="out_proj_ln",
    )(x, conv_out, attn_out, ret_out, w_out, g, b)


def _rope_tables(pos0, t):
    half = RET_HD // 2
    inv = 1.0 / (ROPE_BASE ** (jnp.arange(half, dtype=F32) / half))
    ang = (pos0 + jnp.arange(t, dtype=jnp.int32)).astype(F32)[:, None] * inv[None, :]
    cos = jnp.cos(ang)
    sin = jnp.sin(ang)
    return jnp.concatenate([cos, cos], axis=1), jnp.concatenate([-sin, sin], axis=1)


def _trunk(x, pos0, caches, params, tiles):
    (ln_g, ln_b, wg, wu, wd, w_in, w_out, conv_w, conv_b, conv_ln_g, conv_ln_b, diff_lam, diff_sub_g,
     rel_bias) = params
    bsz, t, d = x.shape
    depth = ln_g.shape[0]
    alpha = (2 * depth) ** 0.25
    tm_ffn, tf, tm_tok, tm_out, t_att = tiles
    prompt = caches is None
    lc = tm_tok
    cos, sin = _rope_tables(pos0, t)
    decay, wq_tab, wk_tab, gl_tab = _retention_tables(lc)
    x = x.reshape(bsz * t, d)
    conv_l, s_l = [], []
    kv_stacks = tuple(jnp.zeros((depth, bsz * t, N_DIFF, DIFF_VD), F32) for _ in range(2))
    for l in range(depth):
        lam_init = 0.8 - 0.6 * math.exp(-0.3 * l)
        if prompt:
            conv_prev = jnp.zeros((bsz, CONV_WIDTH - 1, D_CONV), F32)
            ret_prev = jnp.zeros((bsz, N_RET, RET_HD, RET_HD), F32)
        else:
            cache_conv, cache_k, cache_v, state_ret = caches
            conv_prev, ret_prev = cache_conv[l], state_ret[l]
        x = _ffn_ln(x, wg, wu, wd, ln_g[l, 0:1], ln_b[l, 0:1], l, 0, alpha, tm_ffn, tf)
        u, q, k_stack, kb, v_stack, vb, vt, rq, rk, rkw, rv, gate = _in_proj(
            x, w_in, l, depth, kv_stacks, cos, sin, wk_tab, bsz, t, tm_tok)
        kv_stacks = (k_stack, v_stack)
        if prompt:
            attn_out = _attn_prompt(q, kb, vt, rel_bias, diff_lam, diff_sub_g, l, lam_init, bsz, t, t_att)
        else:
            attn_out = _attn_sample(q, kb, vb, cache_k, cache_v, rel_bias, diff_lam, diff_sub_g, l, lam_init, bsz, t)
        conv_out, conv_new, ret_out, s_new = _conv_retention(
            u, conv_prev, conv_w, conv_b, conv_ln_g, conv_ln_b, rq, rk, rkw, rv, gate, ret_prev, decay, wq_tab, gl_tab,
            l, bsz, t, tm_tok)
        x = _out_proj_ln(x, conv_out, attn_out, ret_out, w_out, ln_g[l, 1:2], ln_b[l, 1:2], l, alpha, tm_out)
        x = _ffn_ln(x, wg, wu, wd, ln_g[l, 2:3], ln_b[l, 2:3], l, 1, alpha, tm_ffn, tf)
        conv_l.append(conv_new)
        s_l.append(s_new)
    k_new, v_new = (a.reshape(depth, bsz, t, N_DIFF, DIFF_VD) for a in kv_stacks)
    return x.reshape(bsz, t, d), jnp.stack(conv_l), k_new, v_new, jnp.stack(s_l)


def _pick(n, cap):
    if n <= cap:
        return n
    best = None
    for c in range(128, cap + 1, 128):
        if n % c == 0:
            best = c
    assert best is not None
    return best


def kernel(x_prompt, x_sample, cache_conv, cache_k, cache_v, state_ret, ln_g, ln_b, ffn_w_gate, ffn_w_up,
           ffn_w_down, w_in, w_out, conv_w, conv_b, conv_ln_g, conv_ln_b, diff_lam, diff_sub_g, rel_bias):
    tf = _pick(ffn_w_gate.shape[-1], 512)
    params = (ln_g, ln_b, ffn_w_gate.astype(BF16), ffn_w_up.astype(BF16), ffn_w_down.astype(BF16),
              w_in.astype(BF16), w_out.astype(BF16), conv_w, conv_b, conv_ln_g, conv_ln_b, diff_lam, diff_sub_g,
              rel_bias)
    bp, tp, _ = x_prompt.shape
    bs, ts, _ = x_sample.shape
    tok_p = _pick(tp, 256)
    y_p, conv_p, k_p, v_p, ret_p = _trunk(
        x_prompt, 0, None, params, (_pick(bp * tp, 512), tf, tok_p, _pick(bp * tp, 512), _pick(tp, 512)))
    y_s, conv_s, k_s, v_s, ret_s = _trunk(
        x_sample, cache_k.shape[2], (cache_conv, cache_k, cache_v, state_ret), params,
        (_pick(bs * ts, 512), tf, ts, _pick(bs * ts, 512), None))
    return (y_p, y_s, conv_p, k_p, v_p, ret_p, conv_s, k_s, v_s, ret_s)
```
